```python
import jax, jax.numpy as jnp
from jax import lax
import numpy as np

D_MODEL = 2048
BATCH = 1
SEQ = 8192
DEPTH = 1
DEC_BATCH = 128
DEC_SEQ = 1
PAST_LEN = 2048
PAGE_SIZE = 128

MIX_WIDTH = D_MODEL
HEAD_DIM = 128
ATTN_HEADS = 8
ATTN_WIDTH = ATTN_HEADS * HEAD_DIM
CONV_WIDTH = MIX_WIDTH - ATTN_WIDTH
CONV_GROUPS = 8
CONV_K = 3
MOBA_BLOCK = 256
MOBA_TOPK = 3
Q_CHUNK = 128
D_FF = 5632
PLE_DIM = 256
RMS_EPS = 1e-6
IN_COLS = 3 * ATTN_WIDTH + 3 * CONV_WIDTH

kernel_name = "hymba_conv_moba_macaron_step"


def rms_norm(x, g):
    xf = x.astype(jnp.float32)
    var = jnp.mean(xf * xf, axis=-1, keepdims=True)
    return (xf * lax.rsqrt(var + RMS_EPS)).astype(x.dtype) * g


def swiglu(x, w_gate, w_up, w_down):
    return (jax.nn.silu(x @ w_gate) * (x @ w_up)) @ w_down


def short_conv(b, c, u, w_conv, conv_state):
    z = c * u
    zp = jnp.concatenate([conv_state.astype(z.dtype), z], axis=1)
    T = z.shape[1]
    y = zp[:, 0:T] * w_conv[0]
    for j in range(1, CONV_K):
        y = y + zp[:, j:j + T] * w_conv[j]
    return b * y, zp[:, -(CONV_K - 1):]


def moba_attention(q, k, v, q_start):
    B, Tq, H, Dh = q.shape
    L = k.shape[1]
    n_blk = -(-L // MOBA_BLOCK)
    pad = n_blk * MOBA_BLOCK - L
    kb = jnp.pad(k, ((0, 0), (0, pad), (0, 0), (0, 0))).reshape(B, n_blk, MOBA_BLOCK, H, Dh)
    vb = jnp.pad(v, ((0, 0), (0, pad), (0, 0), (0, 0))).reshape(B, n_blk, MOBA_BLOCK, H, Dh)
    k_mean = jnp.mean(kb.astype(jnp.float32), axis=2)
    n_sel = min(MOBA_TOPK, n_blk)
    qc = Q_CHUNK if Tq % Q_CHUNK == 0 else Tq
    n_chunks = Tq // qc
    scale = HEAD_DIM ** -0.5
    bi = jnp.arange(B)[:, None, None, None]
    hi = jnp.arange(H)[None, None, :, None]
    blk_ids = jnp.arange(n_blk)
    key_off = jnp.arange(MOBA_BLOCK)

    def chunk_fn(args):
        q_c, pos = args
        own = pos // MOBA_BLOCK
        gate = jnp.einsum('bqhd,bnhd->bqhn', q_c.astype(jnp.float32), k_mean)
        past = blk_ids[None, :] < own[:, None]
        gate = jnp.where(past[None, :, None, :], gate, -jnp.inf)
        _, sel = lax.top_k(gate, n_sel)
        sel_ok = sel < own[None, :, None, None]
        own_b = jnp.broadcast_to(own[None, :, None, None], (B, qc, H, 1))
        idx = jnp.concatenate([sel, own_b], axis=-1)
        ok = jnp.concatenate([sel_ok, jnp.ones_like(own_b, dtype=bool)], axis=-1)
        k_g = kb[bi, idx, :, hi]
        s = jnp.einsum('bqhd,bqhskd->bqhsk', q_c, k_g).astype(jnp.float32) * scale
        key_pos = idx[..., None] * MOBA_BLOCK + key_off
        valid = ok[..., None] & (key_pos <= pos[None, :, None, None, None])
        s = jnp.where(valid, s, -jnp.inf)
        w = jax.nn.softmax(s.reshape(B, qc, H, -1), axis=-1).reshape(s.shape)
        v_g = vb[bi, idx, :, hi]
        return jnp.einsum('bqhsk,bqhskd->bqhd', w.astype(v.dtype), v_g)

    qs = q.reshape(B, n_chunks, qc, H, Dh).transpose(1, 0, 2, 3, 4)
    pos_all = (q_start + jnp.arange(Tq)).reshape(n_chunks, qc)
    out = lax.map(chunk_fn, (qs, pos_all))
    return out.transpose(1, 0, 2, 3, 4).reshape(B, Tq, H * Dh)


def decoder_layer(x, p, past_k, past_v, conv_state, q_start,
                  w_in, w_out, w_conv, w1g, w1u, w1d, w2g, w2u, w2d, w_pg, w_pp,
                  g1a, g1b, gma, gmb, g2a, g2b, gp):
    B, T, _ = x.shape
    x = x + 0.5 * rms_norm(swiglu(rms_norm(x, g1a), w1g, w1u, w1d), g1b)
    h = rms_norm(x, gma)
    proj = h @ w_in
    A, C = ATTN_WIDTH, CONV_WIDTH
    q, k, v, cb, cc, cu = jnp.split(proj, [A, 2 * A, 3 * A, 3 * A + C, 3 * A + 2 * C], axis=-1)
    q = q.reshape(B, T, ATTN_HEADS, HEAD_DIM)
    k = k.reshape(B, T, ATTN_HEADS, HEAD_DIM)
    v = v.reshape(B, T, ATTN_HEADS, HEAD_DIM)
    if past_k is None:
        k_all, v_all = k, v
    else:
        k_all = jnp.concatenate([past_k.astype(k.dtype), k], axis=1)
        v_all = jnp.concatenate([past_v.astype(v.dtype), v], axis=1)
    attn = moba_attention(q, k_all, v_all, q_start)
    conv, new_conv = short_conv(cb, cc, cu, w_conv, conv_state)
    mix = jnp.concatenate([attn, conv], axis=-1) @ w_out
    x = x + rms_norm(mix, gmb)
    x = x + 0.5 * rms_norm(swiglu(rms_norm(x, g2a), w2g, w2u, w2d), g2b)
    x = x + jax.nn.sigmoid(rms_norm(x, gp) @ w_pg) * (p @ w_pp)
    return x, k, v, new_conv


def setup_inputs(seed: int = 0) -> dict:
    key = jax.random.key(seed)
    ks = jax.random.split(key, 32)
    n_pages = PAST_LEN // PAGE_SIZE
    n_used = DEC_BATCH * n_pages
    n_phys = n_used + n_used // 4
    f32 = jnp.float32

    def nrm(k, shape, scale=1.0):
        return jax.random.normal(k, shape, f32) * scale

    def gain(k):
        return 1.0 + 0.05 * jax.random.normal(k, (DEPTH, D_MODEL), f32)

    page_table = jax.random.permutation(ks[0], n_phys)[:n_used].reshape(DEC_BATCH, n_pages).astype(jnp.int32)
    return {
        "x_prompt": nrm(ks[1], (BATCH, SEQ, D_MODEL)),
        "x_sample": nrm(ks[2], (DEC_BATCH, DEC_SEQ, D_MODEL)),
        "cache_k": nrm(ks[3], (DEPTH, n_phys, PAGE_SIZE, ATTN_HEADS, HEAD_DIM)),
        "cache_v": nrm(ks[4], (DEPTH, n_phys, PAGE_SIZE, ATTN_HEADS, HEAD_DIM)),
        "state_conv": nrm(ks[5], (DEPTH, DEC_BATCH, CONV_K - 1, CONV_WIDTH)),
        "page_table": page_table,
        "p_prompt": nrm(ks[6], (DEPTH, BATCH, SEQ, PLE_DIM)),
        "p_sample": nrm(ks[7], (DEPTH, DEC_BATCH, DEC_SEQ, PLE_DIM)),
        "w_in": nrm(ks[8], (DEPTH, D_MODEL, IN_COLS), D_MODEL ** -0.5),
        "w_out": nrm(ks[9], (DEPTH, MIX_WIDTH, D_MODEL), MIX_WIDTH ** -0.5),
        "w_conv": nrm(ks[10], (DEPTH, CONV_K, CONV_WIDTH), CONV_K ** -0.5),
        "w_ffn1_gate": nrm(ks[11], (DEPTH, D_MODEL, D_FF), D_MODEL ** -0.5),
        "w_ffn1_up": nrm(ks[12], (DEPTH, D_MODEL, D_FF), D_MODEL ** -0.5),
        "w_ffn1_down": nrm(ks[13], (DEPTH, D_FF, D_MODEL), D_FF ** -0.5),
        "w_ffn2_gate": nrm(ks[14], (DEPTH, D_MODEL, D_FF), D_MODEL ** -0.5),
        "w_ffn2_up": nrm(ks[15], (DEPTH, D_MODEL, D_FF), D_MODEL ** -0.5),
        "w_ffn2_down": nrm(ks[16], (DEPTH, D_FF, D_MODEL), D_FF ** -0.5),
        "w_ple_gate": nrm(ks[17], (DEPTH, D_MODEL, D_MODEL), D_MODEL ** -0.5),
        "w_ple_proj": nrm(ks[18], (DEPTH, PLE_DIM, D_MODEL), PLE_DIM ** -0.5),
        "g_ffn1_pre": gain(ks[19]),
        "g_ffn1_post": gain(ks[20]),
        "g_mix_pre": gain(ks[21]),
        "g_mix_post": gain(ks[22]),
        "g_ffn2_pre": gain(ks[23]),
        "g_ffn2_post": gain(ks[24]),
        "g_ple": gain(ks[25]),
    }


def reference(x_prompt, x_sample, cache_k, cache_v, state_conv, page_table, p_prompt, p_sample,
              w_in, w_out, w_conv, w_ffn1_gate, w_ffn1_up, w_ffn1_down,
              w_ffn2_gate, w_ffn2_up, w_ffn2_down, w_ple_gate, w_ple_proj,
              g_ffn1_pre, g_ffn1_post, g_mix_pre, g_mix_post, g_ffn2_pre, g_ffn2_post, g_ple):
    n_seq, n_pages = page_table.shape
    page_size = cache_k.shape[2]
    past_len = n_pages * page_size
    xp, xs = x_prompt, x_sample
    kp_l, vp_l, cp_l, ks_l, vs_l, cs_l = [], [], [], [], [], []
    for l in range(DEPTH):
        ws = (w_in[l], w_out[l], w_conv[l], w_ffn1_gate[l], w_ffn1_up[l], w_ffn1_down[l],
              w_ffn2_gate[l], w_ffn2_up[l], w_ffn2_down[l], w_ple_gate[l], w_ple_proj[l],
              g_ffn1_pre[l], g_ffn1_post[l], g_mix_pre[l], g_mix_post[l],
              g_ffn2_pre[l], g_ffn2_post[l], g_ple[l])
        conv0 = jnp.zeros((xp.shape[0], CONV_K - 1, CONV_WIDTH), xp.dtype)
        xp, kp, vp, cp = decoder_layer(xp, p_prompt[l], None, None, conv0, 0, *ws)
        past_k = cache_k[l][page_table].reshape(n_seq, past_len, ATTN_HEADS, HEAD_DIM)
        past_v = cache_v[l][page_table].reshape(n_seq, past_len, ATTN_HEADS, HEAD_DIM)
        xs, ks_, vs_, cs_ = decoder_layer(xs, p_sample[l], past_k, past_v, state_conv[l], past_len, *ws)
        kp_l.append(kp); vp_l.append(vp); cp_l.append(cp)
        ks_l.append(ks_); vs_l.append(vs_); cs_l.append(cs_)
    k_prompt = jnp.stack(kp_l)
    v_prompt = jnp.stack(vp_l)
    conv_prompt = jnp.stack(cp_l)
    k_sample = jnp.stack(ks_l)
    v_sample = jnp.stack(vs_l)
    conv_sample = jnp.stack(cs_l)
    return (xp, xs, k_prompt, v_prompt, conv_prompt, k_sample, v_sample, conv_sample)
```

```python
import functools
import math

import jax
import jax.numpy as jnp
from jax import lax
from jax.experimental import pallas as pl
from jax.experimental.pallas import tpu as pltpu

F32 = jnp.float32
BF16 = jnp.bfloat16

D_MODEL = 2048
HEAD_DIM = 128
ATTN_HEADS = 8
ATTN_WIDTH = ATTN_HEADS * HEAD_DIM
CONV_WIDTH = D_MODEL - ATTN_WIDTH
CONV_K = 3
MOBA_BLOCK = 256
MOBA_TOPK = 3
RMS_EPS = 1e-6

V7X_SUBLANES = 8
V7X_VMEM_LIMIT_BYTES = 56 * 1024 * 1024

NEG_BIG = -1e30
EXP2_SCALE = (HEAD_DIM ** -0.5) * math.log2(math.e)


def _params(*semantics):
    return pltpu.CompilerParams(dimension_semantics=semantics,
                                vmem_limit_bytes=V7X_VMEM_LIMIT_BYTES)


def _rms(x, g):
    var = jnp.mean(x * x, axis=-1, keepdims=True)
    return x * lax.rsqrt(var + RMS_EPS) * g


def _dot(a, b):
    return jnp.dot(a, b, preferred_element_type=F32)


def _dot_nt(a, b):
    return lax.dot_general(a, b, (((1,), (1,)), ((), ())), preferred_element_type=F32)


def _split3(x):
    hi = x.astype(BF16)
    r1 = x - hi.astype(F32)
    mid = r1.astype(BF16)
    lo = (r1 - mid.astype(F32)).astype(BF16)
    return hi, mid, lo


def _ffn_kernel(x_ref, gpre_ref, gpost_ref, gnext_ref, wg_ref, wu_ref, wd_ref,
                y_ref, hn_ref, xn_ref, acc_ref):
    f = pl.program_id(1)

    @pl.when(f == 0)
    def _():
        xn_ref[...] = _rms(x_ref[...], gpre_ref[...]).astype(BF16)
        acc_ref[...] = jnp.zeros_like(acc_ref)

    xn = xn_ref[...]
    g = _dot(xn, wg_ref[...])
    u = _dot(xn, wu_ref[...])
    a = (g * (1.0 / (1.0 + jnp.exp(-g))) * u).astype(BF16)
    acc_ref[...] += _dot(a, wd_ref[...])

    @pl.when(f == pl.num_programs(1) - 1)
    def _():
        y = x_ref[...] + 0.5 * _rms(acc_ref[...], gpost_ref[...])
        y_ref[...] = y
        hn_ref[...] = _rms(y, gnext_ref[...]).astype(BF16)


def _ffn(x, g_pre, g_post, g_next, wg, wu, wd, *, tm, tf):
    m, d = x.shape
    d_ff = wg.shape[1]
    row = lambda i, f: (i, 0)
    vec = pl.BlockSpec((1, d), lambda i, f: (0, 0))
    return pl.pallas_call(
        _ffn_kernel,
        grid=(m // tm, d_ff // tf),
        in_specs=[pl.BlockSpec((tm, d), row), vec, vec, vec,
                  pl.BlockSpec((d, tf), lambda i, f: (0, f)),
                  pl.BlockSpec((d, tf), lambda i, f: (0, f)),
                  pl.BlockSpec((tf, d), lambda i, f: (f, 0))],
        out_specs=[pl.BlockSpec((tm, d), row), pl.BlockSpec((tm, d), row)],
        out_shape=[jax.ShapeDtypeStruct((m, d), F32), jax.ShapeDtypeStruct((m, d), BF16)],
        scratch_shapes=[pltpu.VMEM((tm, d), BF16), pltpu.VMEM((tm, d), F32)],
        compiler_params=_params("arbitrary", "arbitrary"),
        name="ffn",
    )(x, g_pre, g_post, g_next, wg, wu, wd)


def _qkv_kernel(h_ref, wq_ref, wk_ref, wv_ref, q_ref, k_ref, v_ref, kb_ref):
    h = h_ref[...]
    q_ref[...] = _dot(h, wq_ref[...])
    k = _dot(h, wk_ref[...])
    k_ref[...] = k
    kb_ref[...] = k.astype(BF16)
    v_ref[...] = _dot(h, wv_ref[...])


def _qkv_proj(hb, w_in, *, tm, tn):
    m, d = hb.shape
    nj = ATTN_WIDTH // tn
    wspec = lambda part: pl.BlockSpec((d, tn), lambda i, j: (0, part * nj + j))
    ospec = pl.BlockSpec((tm, tn), lambda i, j: (i, j))
    f32o = jax.ShapeDtypeStruct((m, ATTN_WIDTH), F32)
    return pl.pallas_call(
        _qkv_kernel,
        grid=(m // tm, nj),
        in_specs=[pl.BlockSpec((tm, d), lambda i, j: (i, 0)), wspec(0), wspec(1), wspec(2)],
        out_specs=[ospec, ospec, ospec, ospec],
        out_shape=[f32o, f32o, f32o, jax.ShapeDtypeStruct((m, ATTN_WIDTH), BF16)],
        compiler_params=_params("arbitrary", "arbitrary"),
        name="qkv_proj",
    )(hb, w_in, w_in, w_in)


def _conv_prompt_kernel(h_ref, wb_ref, wc_ref, wu_ref, wconv_ref, conv_ref, tail_ref, zbuf_ref):
    i = pl.program_id(1)
    tm = h_ref.shape[0]
    halo = V7X_SUBLANES

    @pl.when(i == 0)
    def _():
        zbuf_ref[0:halo, :] = jnp.zeros((halo, zbuf_ref.shape[1]), F32)

    h = h_ref[...]
    z = _dot(h, wc_ref[...]) * _dot(h, wu_ref[...])
    zbuf_ref[halo:halo + tm, :] = z
    w = wconv_ref[...]
    y = zbuf_ref[halo:halo + tm, :] * w[CONV_K - 1:CONV_K, :]
    for j in range(CONV_K - 1):
        off = halo - (CONV_K - 1) + j
        y = y + zbuf_ref[off:off + tm, :] * w[j:j + 1, :]
    conv_ref[...] = (_dot(h, wb_ref[...]) * y).astype(BF16)
    last = zbuf_ref[tm:tm + halo, :]
    zbuf_ref[0:halo, :] = last
    tail_ref[...] = last


def _conv_prompt(hb, w_in, w_conv, *, tm, tc):
    m, d = hb.shape
    nj = CONV_WIDTH // tc
    base = 3 * ATTN_WIDTH // tc
    wspec = lambda part: pl.BlockSpec((d, tc), lambda j, i: (0, base + part * nj + j))
    return pl.pallas_call(
        _conv_prompt_kernel,
        grid=(nj, m // tm),
        in_specs=[pl.BlockSpec((tm, d), lambda j, i: (i, 0)), wspec(0), wspec(1), wspec(2),
                  pl.BlockSpec((CONV_K, tc), lambda j, i: (0, j))],
        out_specs=[pl.BlockSpec((tm, tc), lambda j, i: (i, j)),
                   pl.BlockSpec((V7X_SUBLANES, tc), lambda j, i: (0, j))],
        out_shape=[jax.ShapeDtypeStruct((m, CONV_WIDTH), BF16),
                   jax.ShapeDtypeStruct((V7X_SUBLANES, CONV_WIDTH), F32)],
        scratch_shapes=[pltpu.VMEM((tm + V7X_SUBLANES, tc), F32)],
        compiler_params=_params("arbitrary", "arbitrary"),
        name="conv_prompt",
    )(hb, w_in, w_in, w_in, w_conv)


def _conv_sample_kernel(h_ref, wb_ref, wc_ref, wu_ref, wconv_ref, st_ref, conv_ref, nst_ref):
    h = h_ref[...]
    z = _dot(h, wc_ref[...]) * _dot(h, wu_ref[...])
    w = wconv_ref[...]
    y = z * w[CONV_K - 1:CONV_K, :]
    for j in range(CONV_K - 1):
        y = y + st_ref[j] * w[j:j + 1, :]
    conv_ref[...] = (_dot(h, wb_ref[...]) * y).astype(BF16)
    for j in range(CONV_K - 2):
        nst_ref[j] = st_ref[j + 1]
    nst_ref[CONV_K - 2] = z


def _conv_sample(hb, w_in, w_conv, state_t, *, tc):
    m, d = hb.shape
    nj = CONV_WIDTH // tc
    base = 3 * ATTN_WIDTH // tc
    wspec = lambda part: pl.BlockSpec((d, tc), lambda j: (0, base + part * nj + j))
    sspec = pl.BlockSpec((CONV_K - 1, m, tc), lambda j: (0, 0, j))
    return pl.pallas_call(
        _conv_sample_kernel,
        grid=(nj,),
        in_specs=[pl.BlockSpec((m, d), lambda j: (0, 0)), wspec(0), wspec(1), wspec(2),
                  pl.BlockSpec((CONV_K, tc), lambda j: (0, j)), sspec],
        out_specs=[pl.BlockSpec((m, tc), lambda j: (0, j)), sspec],
        out_shape=[jax.ShapeDtypeStruct((m, CONV_WIDTH), BF16),
                   jax.ShapeDtypeStruct((CONV_K - 1, m, CONV_WIDTH), F32)],
        compiler_params=_params("arbitrary"),
        name="conv_sample",
    )(hb, w_in, w_in, w_in, w_conv, state_t)


def _kmean_kernel(k_ref, km_ref):
    km_ref[...] = jnp.sum(k_ref[...], axis=0, keepdims=True) * (1.0 / MOBA_BLOCK)


def _kmean(k):
    t, w = k.shape
    nb = t // MOBA_BLOCK
    out = pl.pallas_call(
        _kmean_kernel,
        grid=(nb,),
        in_specs=[pl.BlockSpec((MOBA_BLOCK, w), lambda b: (b, 0))],
        out_specs=pl.BlockSpec((None, 1, w), lambda b: (b, 0, 0)),
        out_shape=jax.ShapeDtypeStruct((nb, 1, w), F32),
        compiler_params=_params("arbitrary"),
        name="kmean",
    )(k)
    return out.reshape(nb, w)


def _top_blocks_bias(gate, n_past, axis_len):
    blk = lax.broadcasted_iota(jnp.int32, gate.shape, 0)
    past = blk < n_past
    g = jnp.where(past, gate, -jnp.inf)
    keep = jnp.zeros(gate.shape, F32)
    for _ in range(MOBA_TOPK):
        mx = jnp.max(g, axis=0, keepdims=True)
        idx = jnp.min(jnp.where(g == mx, blk, axis_len), axis=0, keepdims=True)
        pick = blk == idx
        keep = jnp.where(pick, jnp.where(past, 1.0, keep), keep)
        g = jnp.where(pick, -jnp.inf, g)
    return jnp.where(keep > 0.0, 0.0, NEG_BIG)


def _attn_prompt_kernel(q_ref, kb_ref, v_ref, km_ref, o_ref, vt_ref, bias_ref):
    i = pl.program_id(1)
    nb = vt_ref.shape[0]
    blk = MOBA_BLOCK

    @pl.when(i == 0)
    def _():
        for c in range(nb):
            vt_ref[c] = v_ref[c * blk:(c + 1) * blk, :].T.astype(BF16)

    q = q_ref[...]
    qb = q.astype(BF16)

    q_hi, q_mid, _ = _split3(q)
    k_hi, k_mid, _ = _split3(km_ref[...])
    gate = _dot_nt(k_hi, q_hi) + (_dot_nt(k_hi, q_mid) + _dot_nt(k_mid, q_hi))
    bias_ref[...] = _top_blocks_bias(gate, i, nb)

    k_own = kb_ref[pl.ds(pl.multiple_of(i * blk, blk), blk), :]
    s = _dot_nt(k_own, qb)
    key_pos = lax.broadcasted_iota(jnp.int32, s.shape, 0)
    qry_pos = lax.broadcasted_iota(jnp.int32, s.shape, 1)
    s = jnp.where(key_pos <= qry_pos, s, NEG_BIG)
    m0 = jnp.max(s, axis=0, keepdims=True)
    p = jnp.exp2((s - m0) * EXP2_SCALE)
    l0 = jnp.sum(p, axis=0, keepdims=True)
    acc0 = _dot(vt_ref[i], p.astype(BF16))

    def body(j, carry):
        m, l, acc = carry
        kj = kb_ref[pl.ds(pl.multiple_of(j * blk, blk), blk), :]
        s = _dot_nt(kj, qb) + bias_ref[pl.ds(j, 1), :]
        m_new = jnp.maximum(m, jnp.max(s, axis=0, keepdims=True))
        p = jnp.exp2((s - m_new) * EXP2_SCALE)
        alpha = jnp.exp2((m - m_new) * EXP2_SCALE)
        l = alpha * l + jnp.sum(p, axis=0, keepdims=True)
        acc = acc * alpha + _dot(vt_ref[j], p.astype(BF16))
        return m_new, l, acc

    _, l, acc = lax.fori_loop(0, i, body, (m0, l0, acc0))
    o_ref[...] = (acc / l).T.astype(BF16)


def _attn_prompt(q, kb, v, km):
    t = q.shape[0]
    nb = t // MOBA_BLOCK
    return pl.pallas_call(
        _attn_prompt_kernel,
        grid=(ATTN_HEADS, nb),
        in_specs=[pl.BlockSpec((MOBA_BLOCK, HEAD_DIM), lambda h, i: (i, h)),
                  pl.BlockSpec((t, HEAD_DIM), lambda h, i: (0, h)),
                  pl.BlockSpec((t, HEAD_DIM), lambda h, i: (0, h)),
                  pl.BlockSpec((nb, HEAD_DIM), lambda h, i: (0, h))],
        out_specs=pl.BlockSpec((MOBA_BLOCK, HEAD_DIM), lambda h, i: (i, h)),
        out_shape=jax.ShapeDtypeStruct((t, ATTN_WIDTH), BF16),
        scratch_shapes=[pltpu.VMEM((nb, HEAD_DIM, MOBA_BLOCK), BF16),
                        pltpu.VMEM((nb, MOBA_BLOCK), F32)],
        compiler_params=_params("arbitrary", "arbitrary"),
        name="attn_prompt",
    )(q, kb, v, km)


def _attn_sample_kernel(n_pages, pt_ref, q_ref, kn_ref, vn_ref, *refs):
    k_refs = refs[:n_pages]
    v_refs = refs[n_pages:2 * n_pages]
    o_ref = refs[2 * n_pages]
    s_ref = refs[2 * n_pages + 1]
    page = k_refs[0].shape[0]
    pages_per_blk = MOBA_BLOCK // page
    n_blk = n_pages // pages_per_blk
    width = ATTN_WIDTH
    sub = V7X_SUBLANES

    q = q_ref[...]
    head_of_lane = lax.broadcasted_iota(jnp.int32, (sub, width), 1) // HEAD_DIM
    head_row = lax.broadcasted_iota(jnp.int32, (sub, width), 0)
    onehot = head_of_lane == head_row
    e_t = jnp.where(onehot, 1.0, 0.0).astype(BF16)
    q_blk = jnp.where(onehot, q, 0.0).astype(BF16)

    ksums = []
    for pg in range(n_pages):
        kp = k_refs[pg][...]
        s_ref[pg * page:(pg + 1) * page, :] = _dot_nt(kp.astype(BF16), q_blk)
        part = jnp.sum(kp, axis=0, keepdims=True)
        if pg % pages_per_blk == 0:
            ksums.append(part)
        else:
            ksums[-1] = ksums[-1] + part
    kmean = jnp.concatenate(ksums, axis=0) * (1.0 / MOBA_BLOCK)

    g_hi, g_mid, g_lo = _split3(kmean * q)
    gate = _dot_nt(g_hi, e_t) + (_dot_nt(g_mid, e_t) + _dot_nt(g_lo, e_t))
    bias = _top_blocks_bias(gate, n_blk, n_blk)

    kn8 = jnp.where(head_row == 0, kn_ref[...], 0.0).astype(BF16)
    s_new = _dot_nt(kn8, q_blk)[0:1, :]

    m = s_new
    for b in range(n_blk):
        sb = s_ref[b * MOBA_BLOCK:(b + 1) * MOBA_BLOCK, :] + bias[b:b + 1, :]
        m = jnp.maximum(m, jnp.max(sb, axis=0, keepdims=True))
    p_new = jnp.exp2((s_new - m) * EXP2_SCALE)
    l = p_new
    acc = jnp.zeros((sub, width), F32)
    for pg in range(n_pages):
        b = pg // pages_per_blk
        sp = s_ref[pg * page:(pg + 1) * page, :] + bias[b:b + 1, :]
        p = jnp.exp2((sp - m) * EXP2_SCALE)
        l = l + jnp.sum(p, axis=0, keepdims=True)
        p_wide = _dot(p.astype(BF16), e_t)
        w = p_wide * v_refs[pg][...]
        acc = acc + jnp.sum(w.reshape(page // sub, sub, width), axis=0)
    pn8 = jnp.where(lax.broadcasted_iota(jnp.int32, (sub, sub), 0) == 0, p_new, 0.0)
    acc = acc + _dot(pn8.astype(BF16), e_t) * vn_ref[...]
    l8 = jnp.where(lax.broadcasted_iota(jnp.int32, (sub, sub), 0) == 0, l, 0.0)
    l_hi, l_mid, l_lo = _split3(l8)
    l_wide = _dot(l_hi, e_t) + (_dot(l_mid, e_t) + _dot(l_lo, e_t))
    out = jnp.sum(acc, axis=0, keepdims=True) / l_wide[0:1, :]
    o_ref[...] = out.astype(BF16)


def _attn_sample(q, k_new, v_new, cache_k, cache_v, page_table):
    n_seq, n_pages = page_table.shape
    n_phys, page = cache_k.shape[0], cache_k.shape[1]
    width = ATTN_WIDTH
    assert MOBA_BLOCK % page == 0 and (n_pages * page) % MOBA_BLOCK == 0
    assert n_pages * page // MOBA_BLOCK <= V7X_SUBLANES and ATTN_HEADS <= V7X_SUBLANES
    row = pl.BlockSpec((None, 1, width), lambda b, pt: (b, 0, 0))
    page_spec = lambda pg: pl.BlockSpec((None, page, width), lambda b, pt: (pt[b, pg], 0, 0))
    q3, kn3, vn3 = (a.reshape(n_seq, 1, width) for a in (q, k_new, v_new))
    out = pl.pallas_call(
        functools.partial(_attn_sample_kernel, n_pages),
        grid_spec=pltpu.PrefetchScalarGridSpec(
            num_scalar_prefetch=1,
            grid=(n_seq,),
            in_specs=[row, row, row] + [page_spec(pg) for pg in range(n_pages)] * 2,
            out_specs=row,
            scratch_shapes=[pltpu.VMEM((n_pages * page, V7X_SUBLANES), F32)],
        ),
        out_shape=jax.ShapeDtypeStruct((n_seq, 1, width), BF16),
        compiler_params=_params("arbitrary"),
        name="attn_sample",
    )(page_table, q3, kn3, vn3, *([cache_k] * n_pages), *([cache_v] * n_pages))
    return out.reshape(n_seq, width)


def _out_proj_kernel(x_ref, a_ref, c_ref, w_ref, g_ref, y_ref):
    mix = _dot(a_ref[...], w_ref[0:ATTN_WIDTH, :]) + _dot(c_ref[...], w_ref[ATTN_WIDTH:, :])
    y_ref[...] = x_ref[...] + _rms(mix, g_ref[...])


def _out_proj(x, attn, conv, w_out, g_post, *, tm):
    m, d = x.shape
    row = lambda width: pl.BlockSpec((tm, width), lambda i: (i, 0))
    return pl.pallas_call(
        _out_proj_kernel,
        grid=(m // tm,),
        in_specs=[row(d), row(ATTN_WIDTH), row(CONV_WIDTH),
                  pl.BlockSpec((d, d), lambda i: (0, 0)), pl.BlockSpec((1, d), lambda i: (0, 0))],
        out_specs=row(d),
        out_shape=jax.ShapeDtypeStruct((m, d), F32),
        compiler_params=_params("arbitrary"),
        name="out_proj",
    )(x, attn, conv, w_out, g_post)


def _ple_kernel(x_ref, hn_ref, p_ref, wg_ref, wp_ref, y_ref):
    gate = _dot(hn_ref[...], wg_ref[...])
    emb = _dot(p_ref[...].astype(BF16), wp_ref[...])
    y_ref[...] = x_ref[...] + (1.0 / (1.0 + jnp.exp(-gate))) * emb


def _ple(x, hn, p, w_gate, w_proj, *, tm, tn):
    m, d = x.shape
    pd = p.shape[1]
    return pl.pallas_call(
        _ple_kernel,
        grid=(m // tm, d // tn),
        in_specs=[pl.BlockSpec((tm, tn), lambda i, j: (i, j)),
                  pl.BlockSpec((tm, d), lambda i, j: (i, 0)),
                  pl.BlockSpec((tm, pd), lambda i, j: (i, 0)),
                  pl.BlockSpec((d, tn), lambda i, j: (0, j)),
                  pl.BlockSpec((pd, tn), lambda i, j: (0, j))],
        out_specs=pl.BlockSpec((tm, tn), lambda i, j: (i, j)),
        out_shape=jax.ShapeDtypeStruct((m, d), F32),
        compiler_params=_params("arbitrary", "arbitrary"),
        name="ple",
    )(x, hn, p, w_gate, w_proj)


def _tiles(m):
    tm = min(m, 512)
    return dict(tm=tm, tf=512, tn=512)


def kernel(x_prompt, x_sample, cache_k, cache_v, state_conv, page_table, p_prompt, p_sample, w_in, w_out, w_conv, w_ffn1_gate, w_ffn1_up, w_ffn1_down, w_ffn2_gate, w_ffn2_up, w_ffn2_down, w_ple_gate, w_ple_proj, g_ffn1_pre, g_ffn1_post, g_mix_pre, g_mix_post, g_ffn2_pre, g_ffn2_post, g_ple):
    depth = w_in.shape[0]
    assert depth == 1 and x_prompt.shape[0] == 1 and x_sample.shape[1] == 1
    n_seq = x_sample.shape[0]
    seq = x_prompt.shape[1]
    n_phys, page = cache_k.shape[1], cache_k.shape[2]

    bf = lambda w: w[0].astype(BF16)
    w_in_b, w_out_b = bf(w_in), bf(w_out)
    w1g, w1u, w1d = bf(w_ffn1_gate), bf(w_ffn1_up), bf(w_ffn1_down)
    w2g, w2u, w2d = bf(w_ffn2_gate), bf(w_ffn2_up), bf(w_ffn2_down)
    w_pg, w_pp = bf(w_ple_gate), bf(w_ple_proj)
    wc = w_conv[0]

    def dense_pre(x):
        t = _tiles(x.shape[0])
        x1, h = _ffn(x, g_ffn1_pre, g_ffn1_post, g_mix_pre, w1g, w1u, w1d, tm=t["tm"], tf=t["tf"])
        q, k, v, kb = _qkv_proj(h, w_in_b, tm=t["tm"], tn=t["tn"])
        return x1, h, q, k, v, kb

    def dense_post(x1, attn, conv, p):
        t = _tiles(x1.shape[0])
        x2 = _out_proj(x1, attn, conv, w_out_b, g_mix_post, tm=t["tm"])
        x3, hn = _ffn(x2, g_ffn2_pre, g_ffn2_post, g_ple, w2g, w2u, w2d, tm=t["tm"], tf=t["tf"])
        return _ple(x3, hn, p, w_pg, w_pp, tm=t["tm"], tn=t["tn"])

    xp1, hp, qp, kp, vp, kbp = dense_pre(x_prompt[0])
    conv_p, tail_p = _conv_prompt(hp, w_in_b, wc, tm=512, tc=512)
    attn_p = _attn_prompt(qp, kbp, vp, _kmean(kp))
    y_prompt = dense_post(xp1, attn_p, conv_p, p_prompt[0, 0])

    xs1, hs, qs, ks, vs, _ = dense_pre(x_sample[:, 0, :])
    state_t = jnp.transpose(state_conv[0], (1, 0, 2))
    conv_s, nstate_t = _conv_sample(hs, w_in_b, wc, state_t, tc=512)
    ck = cache_k[0].reshape(n_phys, page, ATTN_WIDTH)
    cv = cache_v[0].reshape(n_phys, page, ATTN_WIDTH)
    attn_s = _attn_sample(qs, ks, vs, ck, cv, page_table)
    y_sample = dense_post(xs1, attn_s, conv_s, p_sample[0, :, 0, :])

    heads = (ATTN_HEADS, HEAD_DIM)
    return (y_prompt[None],
            y_sample[:, None, :],
            kp.reshape(1, 1, seq, *heads),
            vp.reshape(1, 1, seq, *heads),
            tail_p[V7X_SUBLANES - (CONV_K - 1):][None, None],
            ks.reshape(1, n_seq, 1, *heads),
            vs.reshape(1, n_seq, 1, *heads),
            jnp.transpose(nstate_t, (1, 0, 2))[None])
```

```python
import functools
import math

import jax
import jax.numpy as jnp
from jax import lax
from jax.experimental import pallas as pl
from jax.experimental.pallas import tpu as pltpu

F32 = jnp.float32
BF16 = jnp.bfloat16

D_MODEL = 2048
HEAD_DIM = 128
ATTN_HEADS = 8
ATTN_WIDTH = ATTN_HEADS * HEAD_DIM
CONV_WIDTH = D_MODEL - ATTN_WIDTH
CONV_K = 3
MOBA_BLOCK = 256
MOBA_TOPK = 3
RMS_EPS = 1e-6

V7X_SUBLANES = 8
V7X_VMEM_LIMIT_BYTES = 56 * 1024 * 1024

ATTN_GROUP_LOG2 = 2
ATTN_GROUP = 1 << ATTN_GROUP_LOG2
ATTN_HEADS_PER_STEP = 2
SAMPLE_CHUNK = 32

NEG_BIG = -1e30
EXP2_SCALE = (HEAD_DIM ** -0.5) * math.log2(math.e)


def _params(*semantics):
    return pltpu.CompilerParams(dimension_semantics=semantics,
                                vmem_limit_bytes=V7X_VMEM_LIMIT_BYTES)


def _rms(x, g):
    var = jnp.mean(x * x, axis=-1, keepdims=True)
    return x * lax.rsqrt(var + RMS_EPS) * g


def _dot(a, b):
    return jnp.dot(a, b, preferred_element_type=F32)


def _dot_nt(a, b):
    return lax.dot_general(a, b, (((1,), (1,)), ((), ())), preferred_element_type=F32)


def _split3(x):
    hi = x.astype(BF16)
    r1 = x - hi.astype(F32)
    mid = r1.astype(BF16)
    lo = (r1 - mid.astype(F32)).astype(BF16)
    return hi, mid, lo


def _ffn_kernel(x_ref, gpre_ref, gpost_ref, gnext_ref, wg_ref, wu_ref, wd_ref,
                y_ref, hn_ref, xn_ref, acc_ref):
    f = pl.program_id(1)

    @pl.when(f == 0)
    def _():
        xn_ref[...] = _rms(x_ref[...], gpre_ref[...]).astype(BF16)
        acc_ref[...] = jnp.zeros_like(acc_ref)

    xn = xn_ref[...]
    g = _dot(xn, wg_ref[...])
    u = _dot(xn, wu_ref[...])
    a = (g * (1.0 / (1.0 + jnp.exp(-g))) * u).astype(BF16)
    acc_ref[...] += _dot(a, wd_ref[...])

    @pl.when(f == pl.num_programs(1) - 1)
    def _():
        y = x_ref[...] + 0.5 * _rms(acc_ref[...], gpost_ref[...])
        y_ref[...] = y
        hn_ref[...] = _rms(y, gnext_ref[...]).astype(BF16)


def _ffn(x, g_pre, g_post, g_next, wg, wu, wd, *, tm, tf):
    m, d = x.shape
    d_ff = wg.shape[1]
    row = lambda i, f: (i, 0)
    vec = pl.BlockSpec((1, d), lambda i, f: (0, 0))
    return pl.pallas_call(
        _ffn_kernel,
        grid=(m // tm, d_ff // tf),
        in_specs=[pl.BlockSpec((tm, d), row), vec, vec, vec,
                  pl.BlockSpec((d, tf), lambda i, f: (0, f)),
                  pl.BlockSpec((d, tf), lambda i, f: (0, f)),
                  pl.BlockSpec((tf, d), lambda i, f: (f, 0))],
        out_specs=[pl.BlockSpec((tm, d), row), pl.BlockSpec((tm, d), row)],
        out_shape=[jax.ShapeDtypeStruct((m, d), F32), jax.ShapeDtypeStruct((m, d), BF16)],
        scratch_shapes=[pltpu.VMEM((tm, d), BF16), pltpu.VMEM((tm, d), F32)],
        compiler_params=_params("arbitrary", "arbitrary"),
        name="ffn",
    )(x, g_pre, g_post, g_next, wg, wu, wd)


def _qkv_kernel(h_ref, wq_ref, wk_ref, wv_ref, q_ref, k_ref, v_ref, kb_ref):
    h = h_ref[...]
    q_ref[...] = _dot(h, wq_ref[...])
    k = _dot(h, wk_ref[...])
    k_ref[...] = k
    kb_ref[...] = k.astype(BF16)
    v_ref[...] = _dot(h, wv_ref[...])


def _qkv_proj(hb, w_in, *, tm, tn):
    m, d = hb.shape
    nj = ATTN_WIDTH // tn
    wspec = lambda part: pl.BlockSpec((d, tn), lambda i, j: (0, part * nj + j))
    ospec = pl.BlockSpec((tm, tn), lambda i, j: (i, j))
    f32o = jax.ShapeDtypeStruct((m, ATTN_WIDTH), F32)
    return pl.pallas_call(
        _qkv_kernel,
        grid=(m // tm, nj),
        in_specs=[pl.BlockSpec((tm, d), lambda i, j: (i, 0)), wspec(0), wspec(1), wspec(2)],
        out_specs=[ospec, ospec, ospec, ospec],
        out_shape=[f32o, f32o, f32o, jax.ShapeDtypeStruct((m, ATTN_WIDTH), BF16)],
        compiler_params=_params("arbitrary", "arbitrary"),
        name="qkv_proj",
    )(hb, w_in, w_in, w_in)


def _conv_prompt_kernel(h_ref, wb_ref, wc_ref, wu_ref, wconv_ref, conv_ref, tail_ref, zbuf_ref):
    i = pl.program_id(1)
    tm = h_ref.shape[0]
    halo = V7X_SUBLANES

    @pl.when(i == 0)
    def _():
        zbuf_ref[0:halo, :] = jnp.zeros((halo, zbuf_ref.shape[1]), F32)

    h = h_ref[...]
    z = _dot(h, wc_ref[...]) * _dot(h, wu_ref[...])
    zbuf_ref[halo:halo + tm, :] = z
    w = wconv_ref[...]
    y = zbuf_ref[halo:halo + tm, :] * w[CONV_K - 1:CONV_K, :]
    for j in range(CONV_K - 1):
        off = halo - (CONV_K - 1) + j
        y = y + zbuf_ref[off:off + tm, :] * w[j:j + 1, :]
    conv_ref[...] = (_dot(h, wb_ref[...]) * y).astype(BF16)
    last = zbuf_ref[tm:tm + halo, :]
    zbuf_ref[0:halo, :] = last
    tail_ref[...] = last


def _conv_prompt(hb, w_in, w_conv, *, tm, tc):
    m, d = hb.shape
    nj = CONV_WIDTH // tc
    base = 3 * ATTN_WIDTH // tc
    wspec = lambda part: pl.BlockSpec((d, tc), lambda j, i: (0, base + part * nj + j))
    return pl.pallas_call(
        _conv_prompt_kernel,
        grid=(nj, m // tm),
        in_specs=[pl.BlockSpec((tm, d), lambda j, i: (i, 0)), wspec(0), wspec(1), wspec(2),
                  pl.BlockSpec((CONV_K, tc), lambda j, i: (0, j))],
        out_specs=[pl.BlockSpec((tm, tc), lambda j, i: (i, j)),
                   pl.BlockSpec((V7X_SUBLANES, tc), lambda j, i: (0, j))],
        out_shape=[jax.ShapeDtypeStruct((m, CONV_WIDTH), BF16),
                   jax.ShapeDtypeStruct((V7X_SUBLANES, CONV_WIDTH), F32)],
        scratch_shapes=[pltpu.VMEM((tm + V7X_SUBLANES, tc), F32)],
        compiler_params=_params("arbitrary", "arbitrary"),
        name="conv_prompt",
    )(hb, w_in, w_in, w_in, w_conv)


def _conv_sample_kernel(h_ref, wb_ref, wc_ref, wu_ref, wconv_ref, st_ref, conv_ref, nst_ref):
    h = h_ref[...]
    z = _dot(h, wc_ref[...]) * _dot(h, wu_ref[...])
    w = wconv_ref[...]
    y = z * w[CONV_K - 1:CONV_K, :]
    for j in range(CONV_K - 1):
        y = y + st_ref[j] * w[j:j + 1, :]
    conv_ref[...] = (_dot(h, wb_ref[...]) * y).astype(BF16)
    for j in range(CONV_K - 2):
        nst_ref[j] = st_ref[j + 1]
    nst_ref[CONV_K - 2] = z


def _conv_sample(hb, w_in, w_conv, state_t, *, tc):
    m, d = hb.shape
    nj = CONV_WIDTH // tc
    base = 3 * ATTN_WIDTH // tc
    wspec = lambda part: pl.BlockSpec((d, tc), lambda j: (0, base + part * nj + j))
    sspec = pl.BlockSpec((CONV_K - 1, m, tc), lambda j: (0, 0, j))
    return pl.pallas_call(
        _conv_sample_kernel,
        grid=(nj,),
        in_specs=[pl.BlockSpec((m, d), lambda j: (0, 0)), wspec(0), wspec(1), wspec(2),
                  pl.BlockSpec((CONV_K, tc), lambda j: (0, j)), sspec],
        out_specs=[pl.BlockSpec((m, tc), lambda j: (0, j)), sspec],
        out_shape=[jax.ShapeDtypeStruct((m, CONV_WIDTH), BF16),
                   jax.ShapeDtypeStruct((CONV_K - 1, m, CONV_WIDTH), F32)],
        compiler_params=_params("arbitrary"),
        name="conv_sample",
    )(hb, w_in, w_in, w_in, w_conv, state_t)


def _kmean_kernel(k_ref, km_ref):
    nb, w = km_ref.shape
    k = k_ref[...].reshape(nb, MOBA_BLOCK, w)
    km_ref[...] = jnp.sum(k, axis=1) * (1.0 / MOBA_BLOCK)


def _kmean(k):
    t, w = k.shape
    nb = t // MOBA_BLOCK
    per_step = V7X_SUBLANES
    return pl.pallas_call(
        _kmean_kernel,
        grid=(nb // per_step,),
        in_specs=[pl.BlockSpec((per_step * MOBA_BLOCK, w), lambda b: (b, 0))],
        out_specs=pl.BlockSpec((per_step, w), lambda b: (b, 0)),
        out_shape=jax.ShapeDtypeStruct((nb, w), F32),
        compiler_params=_params("arbitrary"),
        name="kmean",
    )(k)


def _top_blocks_bias(gate, n_past, axis_len):
    blk = lax.broadcasted_iota(jnp.int32, gate.shape, 0)
    past = blk < n_past
    g = jnp.where(past, gate, -jnp.inf)
    keep = jnp.zeros(gate.shape, F32)
    for _ in range(MOBA_TOPK):
        mx = jnp.max(g, axis=0, keepdims=True)
        idx = jnp.min(jnp.where(g == mx, blk, axis_len), axis=0, keepdims=True)
        pick = blk == idx
        keep = jnp.where(pick, jnp.where(past, 1.0, keep), keep)
        g = jnp.where(pick, -jnp.inf, g)
    return jnp.where(keep > 0.0, 0.0, NEG_BIG)


def _attn_prompt_kernel(q_ref, kb_ref, v_ref, km_ref, o_ref, vt_ref, vtb_ref, bias_ref):
    i = pl.program_id(1)
    n_heads, nb = bias_ref.shape[0], bias_ref.shape[1]
    blk = MOBA_BLOCK
    grp = ATTN_GROUP
    gk = grp * blk
    dh = HEAD_DIM

    @pl.when(i == 0)
    def _():
        for h in range(n_heads):
            for c in range(nb):
                lane0 = (c % grp) * blk
                vt = v_ref[c * blk:(c + 1) * blk, h * dh:(h + 1) * dh].T.astype(BF16)
                vt_ref[h, c // grp, :, lane0:lane0 + blk] = vt
                vtb_ref[h, c] = vt

    qbs, init = [], []
    for h in range(n_heads):
        q = q_ref[:, h * dh:(h + 1) * dh]
        qb = q.astype(BF16)
        qbs.append(qb)

        q_hi, q_mid, _ = _split3(q)
        k_hi, k_mid, _ = _split3(km_ref[:, h * dh:(h + 1) * dh])
        gate = _dot_nt(k_hi, q_hi) + (_dot_nt(k_hi, q_mid) + _dot_nt(k_mid, q_hi))
        bias = _top_blocks_bias(gate, i, nb)
        for b in range(nb):
            bias_ref[h, b] = bias[b:b + 1, :]

        k_own = kb_ref[pl.ds(pl.multiple_of(i * blk, blk), blk), h * dh:(h + 1) * dh]
        s = _dot_nt(k_own, qb)
        key_pos = lax.broadcasted_iota(jnp.int32, s.shape, 0)
        qry_pos = lax.broadcasted_iota(jnp.int32, s.shape, 1)
        s = jnp.where(key_pos <= qry_pos, s, NEG_BIG)
        m0 = jnp.max(s, axis=0, keepdims=True)
        p = jnp.exp2((s - m0) * EXP2_SCALE)
        l0 = jnp.sum(p, axis=0, keepdims=True)
        init.append((m0, l0, _dot(vtb_ref[h, i], p.astype(BF16))))

    def body(t, carry):
        out, scores = [], []
        for h in range(n_heads):
            kg = kb_ref[pl.ds(pl.multiple_of(t * gk, gk), gk), h * dh:(h + 1) * dh]
            scores.append(_dot_nt(kg, qbs[h]).reshape(grp, blk, blk))
        for h in range(n_heads):
            m, l, acc = carry[h]
            s = scores[h] + bias_ref[h, pl.ds(t * grp, grp)]
            m_new = jnp.maximum(m, jnp.max(jnp.max(s, axis=0), axis=0, keepdims=True))
            p = jnp.exp2((s - m_new) * EXP2_SCALE)
            alpha = jnp.exp2((m - m_new) * EXP2_SCALE)
            l = alpha * l + jnp.sum(jnp.sum(p, axis=0), axis=0, keepdims=True)
            acc = acc * alpha + _dot(vt_ref[h, t], p.reshape(gk, blk).astype(BF16))
            out.append((m_new, l, acc))
        return tuple(out)

    n_groups = lax.shift_right_logical(i + (grp - 1), ATTN_GROUP_LOG2)
    final = lax.fori_loop(0, n_groups, body, tuple(init))
    for h in range(n_heads):
        _, l, acc = final[h]
        o_ref[:, h * dh:(h + 1) * dh] = (acc / l).T.astype(BF16)


def _attn_prompt(q, kb, v, km):
    t = q.shape[0]
    nb = t // MOBA_BLOCK
    hp = ATTN_HEADS_PER_STEP
    assert nb % ATTN_GROUP == 0 and ATTN_HEADS % hp == 0
    w = hp * HEAD_DIM
    return pl.pallas_call(
        _attn_prompt_kernel,
        grid=(ATTN_HEADS // hp, nb),
        in_specs=[pl.BlockSpec((MOBA_BLOCK, w), lambda h, i: (i, h)),
                  pl.BlockSpec((t, w), lambda h, i: (0, h)),
                  pl.BlockSpec((t, w), lambda h, i: (0, h)),
                  pl.BlockSpec((nb, w), lambda h, i: (0, h))],
        out_specs=pl.BlockSpec((MOBA_BLOCK, w), lambda h, i: (i, h)),
        out_shape=jax.ShapeDtypeStruct((t, ATTN_WIDTH), BF16),
        scratch_shapes=[pltpu.VMEM((hp, nb // ATTN_GROUP, HEAD_DIM, ATTN_GROUP * MOBA_BLOCK), BF16),
                        pltpu.VMEM((hp, nb, HEAD_DIM, MOBA_BLOCK), BF16),
                        pltpu.VMEM((hp, nb, 1, MOBA_BLOCK), F32)],
        compiler_params=_params("arbitrary", "arbitrary"),
        name="attn_prompt",
    )(q, kb, v, km)


def _attn_sample_kernel(n_pages, pt_ref, q_ref, kn_ref, vn_ref, *refs):
    k_refs = refs[:n_pages]
    v_refs = refs[n_pages:2 * n_pages]
    o_ref = refs[2 * n_pages]
    s_ref = refs[2 * n_pages + 1]
    page = k_refs[0].shape[0]
    pages_per_blk = MOBA_BLOCK // page
    n_blk = n_pages // pages_per_blk
    lanes = (ATTN_HEADS, HEAD_DIM)

    ch = SAMPLE_CHUNK
    chunks = [(pg, c * ch) for pg in range(n_pages) for c in range(page // ch)]

    q = q_ref[...]
    qe = q * EXP2_SCALE

    def score(k):
        return jnp.broadcast_to(jnp.sum(k * qe, axis=-1, keepdims=True), k.shape)

    ksum = [None] * n_blk
    smax = [None] * n_blk
    for pg, r0 in chunks:
        b = pg // pages_per_blk
        k = k_refs[pg][r0:r0 + ch]
        s = score(k)
        s_ref[pg * page + r0:pg * page + r0 + ch] = s
        ks, cm = jnp.sum(k, axis=0), jnp.max(s, axis=0)
        ksum[b] = ks if ksum[b] is None else ksum[b] + ks
        smax[b] = cm if smax[b] is None else jnp.maximum(smax[b], cm)

    kmean = jnp.concatenate([ks[None] for ks in ksum], axis=0) * (1.0 / MOBA_BLOCK)
    gate = jnp.sum(kmean * q[None], axis=-1, keepdims=True)
    bias = _top_blocks_bias(gate, n_blk, n_blk)
    s_new = score(kn_ref[...])
    m = s_new
    for b in range(n_blk):
        m = jnp.maximum(m, smax[b] + bias[b])
    offs = [m - bias[b] for b in range(n_blk)]

    p_new = jnp.exp2(s_new - m)
    l = p_new
    acc = p_new * vn_ref[...]
    for pg, r0 in chunks:
        s = s_ref[pg * page + r0:pg * page + r0 + ch]
        p = jnp.exp2(s - offs[pg // pages_per_blk])
        l = l + jnp.sum(p, axis=0)
        acc = acc + jnp.sum(p * v_refs[pg][r0:r0 + ch], axis=0)
    o_ref[...] = acc / l


def _attn_sample(q, k_new, v_new, cache_k, cache_v, page_table):
    n_seq, n_pages = page_table.shape
    page = cache_k.shape[2]
    assert MOBA_BLOCK % page == 0 and (n_pages * page) % MOBA_BLOCK == 0
    lanes = (ATTN_HEADS, HEAD_DIM)
    row = pl.BlockSpec((None,) + lanes, lambda b, pt: (b, 0, 0))
    page_spec = lambda pg: pl.BlockSpec((None, None, page) + lanes,
                                        lambda b, pt: (0, pt[b, pg], 0, 0, 0))
    return pl.pallas_call(
        functools.partial(_attn_sample_kernel, n_pages),
        grid_spec=pltpu.PrefetchScalarGridSpec(
            num_scalar_prefetch=1,
            grid=(n_seq,),
            in_specs=[row, row, row] + [page_spec(pg) for pg in range(n_pages)] * 2,
            out_specs=row,
            scratch_shapes=[pltpu.VMEM((n_pages * page,) + lanes, F32)],
        ),
        out_shape=jax.ShapeDtypeStruct((n_seq,) + lanes, F32),
        compiler_params=_params("arbitrary"),
        name="attn_sample",
    )(page_table, q, k_new, v_new, *([cache_k] * n_pages), *([cache_v] * n_pages))


def _out_proj_kernel(x_ref, a_ref, c_ref, w_ref, g_ref, y_ref):
    mix = _dot(a_ref[...], w_ref[0:ATTN_WIDTH, :]) + _dot(c_ref[...], w_ref[ATTN_WIDTH:, :])
    y_ref[...] = x_ref[...] + _rms(mix, g_ref[...])


def _out_proj(x, attn, conv, w_out, g_post, *, tm):
    m, d = x.shape
    row = lambda width: pl.BlockSpec((tm, width), lambda i: (i, 0))
    return pl.pallas_call(
        _out_proj_kernel,
        grid=(m // tm,),
        in_specs=[row(d), row(ATTN_WIDTH), row(CONV_WIDTH),
                  pl.BlockSpec((d, d), lambda i: (0, 0)), pl.BlockSpec((1, d), lambda i: (0, 0))],
        out_specs=row(d),
        out_shape=jax.ShapeDtypeStruct((m, d), F32),
        compiler_params=_params("arbitrary"),
        name="out_proj",
    )(x, attn, conv, w_out, g_post)


def _ple_kernel(x_ref, hn_ref, p_ref, wg_ref, wp_ref, y_ref):
    gate = _dot(hn_ref[...], wg_ref[...])
    emb = _dot(p_ref[...].astype(BF16), wp_ref[...])
    y_ref[...] = x_ref[...] + (1.0 / (1.0 + jnp.exp(-gate))) * emb


def _ple(x, hn, p, w_gate, w_proj, *, tm, tn):
    m, d = x.shape
    pd = p.shape[1]
    return pl.pallas_call(
        _ple_kernel,
        grid=(m // tm, d // tn),
        in_specs=[pl.BlockSpec((tm, tn), lambda i, j: (i, j)),
                  pl.BlockSpec((tm, d), lambda i, j: (i, 0)),
                  pl.BlockSpec((tm, pd), lambda i, j: (i, 0)),
                  pl.BlockSpec((d, tn), lambda i, j: (0, j)),
                  pl.BlockSpec((pd, tn), lambda i, j: (0, j))],
        out_specs=pl.BlockSpec((tm, tn), lambda i, j: (i, j)),
        out_shape=jax.ShapeDtypeStruct((m, d), F32),
        compiler_params=_params("arbitrary", "arbitrary"),
        name="ple",
    )(x, hn, p, w_gate, w_proj)


def _tiles(m):
    tm = min(m, 512)
    return dict(tm=tm, tf=512, tn=512)


def kernel(x_prompt, x_sample, cache_k, cache_v, state_conv, page_table, p_prompt, p_sample, w_in, w_out, w_conv, w_ffn1_gate, w_ffn1_up, w_ffn1_down, w_ffn2_gate, w_ffn2_up, w_ffn2_down, w_ple_gate, w_ple_proj, g_ffn1_pre, g_ffn1_post, g_mix_pre, g_mix_post, g_ffn2_pre, g_ffn2_post, g_ple):
    depth = w_in.shape[0]
    assert depth == 1 and x_prompt.shape[0] == 1 and x_sample.shape[1] == 1
    n_seq = x_sample.shape[0]
    seq = x_prompt.shape[1]
    n_phys, page = cache_k.shape[1], cache_k.shape[2]

    bf = lambda w: w[0].astype(BF16)
    w_in_b, w_out_b = bf(w_in), bf(w_out)
    w1g, w1u, w1d = bf(w_ffn1_gate), bf(w_ffn1_up), bf(w_ffn1_down)
    w2g, w2u, w2d = bf(w_ffn2_gate), bf(w_ffn2_up), bf(w_ffn2_down)
    w_pg, w_pp = bf(w_ple_gate), bf(w_ple_proj)
    wc = w_conv[0]

    def dense_pre(x):
        t = _tiles(x.shape[0])
        x1, h = _ffn(x, g_ffn1_pre, g_ffn1_post, g_mix_pre, w1g, w1u, w1d, tm=t["tm"], tf=t["tf"])
        q, k, v, kb = _qkv_proj(h, w_in_b, tm=t["tm"], tn=t["tn"])
        return x1, h, q, k, v, kb

    def dense_post(x1, attn, conv, p):
        t = _tiles(x1.shape[0])
        x2 = _out_proj(x1, attn, conv, w_out_b, g_mix_post, tm=t["tm"])
        x3, hn = _ffn(x2, g_ffn2_pre, g_ffn2_post, g_ple, w2g, w2u, w2d, tm=t["tm"], tf=t["tf"])
        return _ple(x3, hn, p, w_pg, w_pp, tm=t["tm"], tn=t["tn"])

    xp1, hp, qp, kp, vp, kbp = dense_pre(x_prompt[0])
    conv_p, tail_p = _conv_prompt(hp, w_in_b, wc, tm=512, tc=512)
    attn_p = _attn_prompt(qp, kbp, vp, _kmean(kp))
    y_prompt = dense_post(xp1, attn_p, conv_p, p_prompt[0, 0])

    xs1, hs, qs, ks, vs, _ = dense_pre(x_sample[:, 0, :])
    state_t = jnp.transpose(state_conv[0], (1, 0, 2))
    conv_s, nstate_t = _conv_sample(hs, w_in_b, wc, state_t, tc=512)
    heads = (ATTN_HEADS, HEAD_DIM)
    qs3, ks3, vs3 = (a.reshape(n_seq, *heads) for a in (qs, ks, vs))
    attn_s = _attn_sample(qs3, ks3, vs3, cache_k, cache_v, page_table)
    attn_s = attn_s.reshape(n_seq, ATTN_WIDTH).astype(BF16)
    y_sample = dense_post(xs1, attn_s, conv_s, p_sample[0, :, 0, :])

    return (y_prompt[None],
            y_sample[:, None, :],
            kp.reshape(1, 1, seq, *heads),
            vp.reshape(1, 1, seq, *heads),
            tail_p[V7X_SUBLANES - (CONV_K - 1):][None, None],
            ks.reshape(1, n_seq, 1, *heads),
            vs.reshape(1, n_seq, 1, *heads),
            jnp.transpose(nstate_t, (1, 0, 2))[None])
```

```python
import functools
import math

import jax
import jax.numpy as jnp
from jax import lax
from jax.experimental import pallas as pl
from jax.experimental.pallas import tpu as pltpu

F32 = jnp.float32
BF16 = jnp.bfloat16

D_MODEL = 2048
HEAD_DIM = 128
ATTN_HEADS = 8
ATTN_WIDTH = ATTN_HEADS * HEAD_DIM
CONV_WIDTH = D_MODEL - ATTN_WIDTH
CONV_K = 3
MOBA_BLOCK = 256
MOBA_TOPK = 3
RMS_EPS = 1e-6

V7X_SUBLANES = 8
V7X_VMEM_LIMIT_BYTES = 56 * 1024 * 1024

ATTN_GROUP_LOG2 = 2
ATTN_GROUP = 1 << ATTN_GROUP_LOG2
ATTN_HEADS_PER_STEP = 2
SAMPLE_CHUNK = 8

NEG_BIG = -1e30
EXP2_SCALE = (HEAD_DIM ** -0.5) * math.log2(math.e)


def _params(*semantics):
    return pltpu.CompilerParams(dimension_semantics=semantics,
                                vmem_limit_bytes=V7X_VMEM_LIMIT_BYTES)


def _rms(x, g):
    var = jnp.mean(x * x, axis=-1, keepdims=True)
    return x * lax.rsqrt(var + RMS_EPS) * g


def _dot(a, b):
    return jnp.dot(a, b, preferred_element_type=F32)


def _dot_nt(a, b):
    return lax.dot_general(a, b, (((1,), (1,)), ((), ())), preferred_element_type=F32)


def _split3(x):
    hi = x.astype(BF16)
    r1 = x - hi.astype(F32)
    mid = r1.astype(BF16)
    lo = (r1 - mid.astype(F32)).astype(BF16)
    return hi, mid, lo


def _ffn_kernel(x_ref, gpre_ref, gpost_ref, gnext_ref, wg_ref, wu_ref, wd_ref,
                y_ref, hn_ref, xn_ref, acc_ref):
    f = pl.program_id(1)

    @pl.when(f == 0)
    def _():
        xn_ref[...] = _rms(x_ref[...], gpre_ref[...]).astype(BF16)
        acc_ref[...] = jnp.zeros_like(acc_ref)

    xn = xn_ref[...]
    g = _dot(xn, wg_ref[...])
    u = _dot(xn, wu_ref[...])
    a = (g * (1.0 / (1.0 + jnp.exp(-g))) * u).astype(BF16)
    acc_ref[...] += _dot(a, wd_ref[...])

    @pl.when(f == pl.num_programs(1) - 1)
    def _():
        y = x_ref[...] + 0.5 * _rms(acc_ref[...], gpost_ref[...])
        y_ref[...] = y
        hn_ref[...] = _rms(y, gnext_ref[...]).astype(BF16)


def _ffn(x, g_pre, g_post, g_next, wg, wu, wd, *, tm, tf):
    m, d = x.shape
    d_ff = wg.shape[1]
    row = lambda i, f: (i, 0)
    vec = pl.BlockSpec((1, d), lambda i, f: (0, 0))
    return pl.pallas_call(
        _ffn_kernel,
        grid=(m // tm, d_ff // tf),
        in_specs=[pl.BlockSpec((tm, d), row), vec, vec, vec,
                  pl.BlockSpec((d, tf), lambda i, f: (0, f)),
                  pl.BlockSpec((d, tf), lambda i, f: (0, f)),
                  pl.BlockSpec((tf, d), lambda i, f: (f, 0))],
        out_specs=[pl.BlockSpec((tm, d), row), pl.BlockSpec((tm, d), row)],
        out_shape=[jax.ShapeDtypeStruct((m, d), F32), jax.ShapeDtypeStruct((m, d), BF16)],
        scratch_shapes=[pltpu.VMEM((tm, d), BF16), pltpu.VMEM((tm, d), F32)],
        compiler_params=_params("arbitrary", "arbitrary"),
        name="ffn",
    )(x, g_pre, g_post, g_next, wg, wu, wd)


def _qkv_kernel(h_ref, wq_ref, wk_ref, wv_ref, q_ref, k_ref, v_ref, kb_ref):
    h = h_ref[...]
    q_ref[...] = _dot(h, wq_ref[...])
    k = _dot(h, wk_ref[...])
    k_ref[...] = k
    kb_ref[...] = k.astype(BF16)
    v_ref[...] = _dot(h, wv_ref[...])


def _qkv_proj(hb, w_in, *, tm, tn):
    m, d = hb.shape
    nj = ATTN_WIDTH // tn
    wspec = lambda part: pl.BlockSpec((d, tn), lambda i, j: (0, part * nj + j))
    ospec = pl.BlockSpec((tm, tn), lambda i, j: (i, j))
    f32o = jax.ShapeDtypeStruct((m, ATTN_WIDTH), F32)
    return pl.pallas_call(
        _qkv_kernel,
        grid=(m // tm, nj),
        in_specs=[pl.BlockSpec((tm, d), lambda i, j: (i, 0)), wspec(0), wspec(1), wspec(2)],
        out_specs=[ospec, ospec, ospec, ospec],
        out_shape=[f32o, f32o, f32o, jax.ShapeDtypeStruct((m, ATTN_WIDTH), BF16)],
        compiler_params=_params("arbitrary", "arbitrary"),
        name="qkv_proj",
    )(hb, w_in, w_in, w_in)


def _conv_prompt_kernel(h_ref, wb_ref, wc_ref, wu_ref, wconv_ref, conv_ref, tail_ref, zbuf_ref):
    i = pl.program_id(1)
    tm = h_ref.shape[0]
    halo = V7X_SUBLANES

    @pl.when(i == 0)
    def _():
        zbuf_ref[0:halo, :] = jnp.zeros((halo, zbuf_ref.shape[1]), F32)

    h = h_ref[...]
    z = _dot(h, wc_ref[...]) * _dot(h, wu_ref[...])
    zbuf_ref[halo:halo + tm, :] = z
    w = wconv_ref[...]
    y = zbuf_ref[halo:halo + tm, :] * w[CONV_K - 1:CONV_K, :]
    for j in range(CONV_K - 1):
        off = halo - (CONV_K - 1) + j
        y = y + zbuf_ref[off:off + tm, :] * w[j:j + 1, :]
    conv_ref[...] = (_dot(h, wb_ref[...]) * y).astype(BF16)
    last = zbuf_ref[tm:tm + halo, :]
    zbuf_ref[0:halo, :] = last
    tail_ref[...] = last


def _conv_prompt(hb, w_in, w_conv, *, tm, tc):
    m, d = hb.shape
    nj = CONV_WIDTH // tc
    base = 3 * ATTN_WIDTH // tc
    wspec = lambda part: pl.BlockSpec((d, tc), lambda j, i: (0, base + part * nj + j))
    return pl.pallas_call(
        _conv_prompt_kernel,
        grid=(nj, m // tm),
        in_specs=[pl.BlockSpec((tm, d), lambda j, i: (i, 0)), wspec(0), wspec(1), wspec(2),
                  pl.BlockSpec((CONV_K, tc), lambda j, i: (0, j))],
        out_specs=[pl.BlockSpec((tm, tc), lambda j, i: (i, j)),
                   pl.BlockSpec((V7X_SUBLANES, tc), lambda j, i: (0, j))],
        out_shape=[jax.ShapeDtypeStruct((m, CONV_WIDTH), BF16),
                   jax.ShapeDtypeStruct((V7X_SUBLANES, CONV_WIDTH), F32)],
        scratch_shapes=[pltpu.VMEM((tm + V7X_SUBLANES, tc), F32)],
        compiler_params=_params("arbitrary", "arbitrary"),
        name="conv_prompt",
    )(hb, w_in, w_in, w_in, w_conv)


def _conv_sample_kernel(h_ref, wb_ref, wc_ref, wu_ref, wconv_ref, st_ref, conv_ref, nst_ref):
    h = h_ref[...]
    z = _dot(h, wc_ref[...]) * _dot(h, wu_ref[...])
    w = wconv_ref[...]
    y = z * w[CONV_K - 1:CONV_K, :]
    for j in range(CONV_K - 1):
        y = y + st_ref[j] * w[j:j + 1, :]
    conv_ref[...] = (_dot(h, wb_ref[...]) * y).astype(BF16)
    for j in range(CONV_K - 2):
        nst_ref[j] = st_ref[j + 1]
    nst_ref[CONV_K - 2] = z


def _conv_sample(hb, w_in, w_conv, state_t, *, tc):
    m, d = hb.shape
    nj = CONV_WIDTH // tc
    base = 3 * ATTN_WIDTH // tc
    wspec = lambda part: pl.BlockSpec((d, tc), lambda j: (0, base + part * nj + j))
    sspec = pl.BlockSpec((CONV_K - 1, m, tc), lambda j: (0, 0, j))
    return pl.pallas_call(
        _conv_sample_kernel,
        grid=(nj,),
        in_specs=[pl.BlockSpec((m, d), lambda j: (0, 0)), wspec(0), wspec(1), wspec(2),
                  pl.BlockSpec((CONV_K, tc), lambda j: (0, j)), sspec],
        out_specs=[pl.BlockSpec((m, tc), lambda j: (0, j)), sspec],
        out_shape=[jax.ShapeDtypeStruct((m, CONV_WIDTH), BF16),
                   jax.ShapeDtypeStruct((CONV_K - 1, m, CONV_WIDTH), F32)],
        compiler_params=_params("arbitrary"),
        name="conv_sample",
    )(hb, w_in, w_in, w_in, w_conv, state_t)


def _kmean_kernel(k_ref, km_ref):
    nb, w = km_ref.shape
    k = k_ref[...].reshape(nb, MOBA_BLOCK, w)
    km_ref[...] = jnp.sum(k, axis=1) * (1.0 / MOBA_BLOCK)


def _kmean(k):
    t, w = k.shape
    nb = t // MOBA_BLOCK
    per_step = V7X_SUBLANES
    return pl.pallas_call(
        _kmean_kernel,
        grid=(nb // per_step,),
        in_specs=[pl.BlockSpec((per_step * MOBA_BLOCK, w), lambda b: (b, 0))],
        out_specs=pl.BlockSpec((per_step, w), lambda b: (b, 0)),
        out_shape=jax.ShapeDtypeStruct((nb, w), F32),
        compiler_params=_params("arbitrary"),
        name="kmean",
    )(k)


def _top_blocks_bias(gate, n_past, axis_len):
    blk = lax.broadcasted_iota(jnp.int32, gate.shape, 0)
    past = blk < n_past
    g = jnp.where(past, gate, -jnp.inf)
    keep = jnp.zeros(gate.shape, F32)
    for _ in range(MOBA_TOPK):
        mx = jnp.max(g, axis=0, keepdims=True)
        idx = jnp.min(jnp.where(g == mx, blk, axis_len), axis=0, keepdims=True)
        pick = blk == idx
        keep = jnp.where(pick, jnp.where(past, 1.0, keep), keep)
        g = jnp.where(pick, -jnp.inf, g)
    return jnp.where(keep > 0.0, 0.0, NEG_BIG)


def _attn_prompt_kernel(q_ref, kb_ref, v_ref, km_ref, o_ref, kx_ref, vt_ref, vtb_ref, s_ref, sd_ref):
    i = pl.program_id(1)
    n_heads, n_grp = vt_ref.shape[0], vt_ref.shape[1]
    blk = MOBA_BLOCK
    grp = ATTN_GROUP
    gk = grp * blk
    nb = n_grp * grp
    dh = HEAD_DIM

    @pl.when(i == 0)
    def _():
        col = lax.broadcasted_iota(jnp.int32, (blk, dh), 1)
        for h in range(n_heads):
            for c in range(nb):
                rows = slice(c * blk, (c + 1) * blk)
                lane0 = (c % grp) * blk
                vt = v_ref[rows, h * dh:(h + 1) * dh].T.astype(BF16)
                vt_ref[h, c // grp, :, lane0:lane0 + blk] = vt
                vtb_ref[h, c] = vt
                kx_ref[h, rows, 0:dh] = kb_ref[rows, h * dh:(h + 1) * dh]
                kx_ref[h, rows, dh:2 * dh] = jnp.where(col == c, 1.0, 0.0).astype(BF16)

    def scores(h, g, q_ext):
        kg = kx_ref[h, pl.ds(pl.multiple_of(g * gk, gk), gk), :]
        return _dot_nt(kg, q_ext)

    def update(h, g, slot, state):
        m, l, acc = state
        s = s_ref[h, slot]
        m_new = jnp.maximum(m, jnp.max(s, axis=0, keepdims=True))
        p = jnp.exp2(s - m_new)
        alpha = jnp.exp2(m - m_new)
        l = alpha * l + jnp.sum(p, axis=0, keepdims=True)
        acc = acc * alpha + _dot(vt_ref[h, g], p.astype(BF16))
        return m_new, l, acc

    q_exts, state = [], []
    for h in range(n_heads):
        q = q_ref[:, h * dh:(h + 1) * dh]
        qb = (q * EXP2_SCALE).astype(BF16)

        q_hi, q_mid, _ = _split3(q)
        k_hi, k_mid, _ = _split3(km_ref[:, h * dh:(h + 1) * dh])
        gate = _dot_nt(k_hi, q_hi) + (_dot_nt(k_hi, q_mid) + _dot_nt(k_mid, q_hi))
        bias = _top_blocks_bias(gate, i, nb)
        bias = jnp.concatenate([bias, jnp.zeros((dh - nb, blk), F32)], axis=0)
        q_exts.append(jnp.concatenate([qb, bias.T.astype(BF16)], axis=1))

        k_own = kb_ref[pl.ds(pl.multiple_of(i * blk, blk), blk), h * dh:(h + 1) * dh]
        s = _dot_nt(k_own, qb)
        key_pos = lax.broadcasted_iota(jnp.int32, s.shape, 0)
        qry_pos = lax.broadcasted_iota(jnp.int32, s.shape, 1)
        sd_ref[h] = jnp.where(key_pos <= qry_pos, s, NEG_BIG)
        state.append((jnp.full((1, blk), NEG_BIG, F32), jnp.zeros((1, blk), F32),
                      jnp.zeros((dh, blk), F32)))

    n_groups = lax.shift_right_logical(i + (grp - 1), ATTN_GROUP_LOG2)
    n_pairs = lax.shift_right_logical(n_groups, 1)
    for h in range(n_heads):
        s_ref[h, 0] = scores(h, 0, q_exts[h])

    def body(u, carry):
        g0 = 2 * u
        for h in range(n_heads):
            s_ref[h, 1] = scores(h, g0 + 1, q_exts[h])
        carry = tuple(update(h, g0, 0, carry[h]) for h in range(n_heads))
        for h in range(n_heads):
            s_ref[h, 0] = scores(h, jnp.minimum(g0 + 2, n_grp - 1), q_exts[h])
        return tuple(update(h, g0 + 1, 1, carry[h]) for h in range(n_heads))

    state = lax.fori_loop(0, n_pairs, body, tuple(state))

    g_tail = jnp.minimum(2 * n_pairs, n_grp - 1)
    tail_mask = jnp.where(2 * n_pairs < n_groups, 0.0, NEG_BIG)
    for h in range(n_heads):
        m, l, acc = state[h]
        s = s_ref[h, 0] + tail_mask
        sd = sd_ref[h]
        m_new = jnp.maximum(m, jnp.maximum(jnp.max(s, axis=0, keepdims=True),
                                           jnp.max(sd, axis=0, keepdims=True)))
        p, pd = jnp.exp2(s - m_new), jnp.exp2(sd - m_new)
        alpha = jnp.exp2(m - m_new)
        l = alpha * l + (jnp.sum(p, axis=0, keepdims=True) + jnp.sum(pd, axis=0, keepdims=True))
        acc = acc * alpha + (_dot(vt_ref[h, g_tail], p.astype(BF16))
                             + _dot(vtb_ref[h, i], pd.astype(BF16)))
        o_ref[:, h * dh:(h + 1) * dh] = (acc / l).T.astype(BF16)


def _attn_prompt(q, kb, v, km):
    t = q.shape[0]
    nb = t // MOBA_BLOCK
    hp = ATTN_HEADS_PER_STEP
    assert nb % ATTN_GROUP == 0 and ATTN_HEADS % hp == 0
    w = hp * HEAD_DIM
    return pl.pallas_call(
        _attn_prompt_kernel,
        grid=(ATTN_HEADS // hp, nb),
        in_specs=[pl.BlockSpec((MOBA_BLOCK, w), lambda h, i: (i, h)),
                  pl.BlockSpec((t, w), lambda h, i: (0, h)),
                  pl.BlockSpec((t, w), lambda h, i: (0, h)),
                  pl.BlockSpec((nb, w), lambda h, i: (0, h))],
        out_specs=pl.BlockSpec((MOBA_BLOCK, w), lambda h, i: (i, h)),
        out_shape=jax.ShapeDtypeStruct((t, ATTN_WIDTH), BF16),
        scratch_shapes=[pltpu.VMEM((hp, t, 2 * HEAD_DIM), BF16),
                        pltpu.VMEM((hp, nb // ATTN_GROUP, HEAD_DIM, ATTN_GROUP * MOBA_BLOCK), BF16),
                        pltpu.VMEM((hp, nb, HEAD_DIM, MOBA_BLOCK), BF16),
                        pltpu.VMEM((hp, 2, ATTN_GROUP * MOBA_BLOCK, MOBA_BLOCK), F32),
                        pltpu.VMEM((hp, MOBA_BLOCK, MOBA_BLOCK), F32)],
        compiler_params=_params("arbitrary", "arbitrary"),
        name="attn_prompt",
    )(q, kb, v, km)


def _attn_sample_kernel(n_pages, pt_ref, q_ref, kn_ref, vn_ref, *refs):
    k_refs = refs[:n_pages]
    v_refs = refs[n_pages:2 * n_pages]
    o_ref = refs[2 * n_pages]
    s_ref = refs[2 * n_pages + 1]
    page = k_refs[0].shape[0]
    pages_per_blk = MOBA_BLOCK // page
    n_blk = n_pages // pages_per_blk
    lanes = (ATTN_HEADS, HEAD_DIM)

    ch = SAMPLE_CHUNK
    chunks = [(pg, c * ch) for pg in range(n_pages) for c in range(page // ch)]

    q = q_ref[...]
    qe = q * EXP2_SCALE

    def score(k):
        return jnp.broadcast_to(jnp.sum(k * qe, axis=-1, keepdims=True), k.shape)

    ksum = [None] * n_blk
    smax = [None] * n_blk
    for pg, r0 in chunks:
        b = pg // pages_per_blk
        k = k_refs[pg][r0:r0 + ch]
        s = score(k)
        s_ref[pg * page + r0:pg * page + r0 + ch] = s
        ks, cm = jnp.sum(k, axis=0), jnp.max(s, axis=0)
        ksum[b] = ks if ksum[b] is None else ksum[b] + ks
        smax[b] = cm if smax[b] is None else jnp.maximum(smax[b], cm)

    kmean = jnp.concatenate([ks[None] for ks in ksum], axis=0) * (1.0 / MOBA_BLOCK)
    gate = jnp.sum(kmean * q[None], axis=-1, keepdims=True)
    bias = _top_blocks_bias(gate, n_blk, n_blk)
    s_new = score(kn_ref[...])
    m = s_new
    for b in range(n_blk):
        m = jnp.maximum(m, smax[b] + bias[b])
    offs = [m - bias[b] for b in range(n_blk)]

    p_new = jnp.exp2(s_new - m)
    l = p_new
    acc = p_new * vn_ref[...]
    for pg, r0 in chunks:
        s = s_ref[pg * page + r0:pg * page + r0 + ch]
        p = jnp.exp2(s - offs[pg // pages_per_blk])
        l = l + jnp.sum(p, axis=0)
        acc = acc + jnp.sum(p * v_refs[pg][r0:r0 + ch], axis=0)
    o_ref[...] = acc / l


def _attn_sample(q, k_new, v_new, cache_k, cache_v, page_table):
    n_seq, n_pages = page_table.shape
    page = cache_k.shape[2]
    assert MOBA_BLOCK % page == 0 and (n_pages * page) % MOBA_BLOCK == 0
    lanes = (ATTN_HEADS, HEAD_DIM)
    row = pl.BlockSpec((None,) + lanes, lambda b, pt: (b, 0, 0))
    page_spec = lambda pg: pl.BlockSpec((None, None, page) + lanes,
                                        lambda b, pt: (0, pt[b, pg], 0, 0, 0))
    return pl.pallas_call(
        functools.partial(_attn_sample_kernel, n_pages),
        grid_spec=pltpu.PrefetchScalarGridSpec(
            num_scalar_prefetch=1,
            grid=(n_seq,),
            in_specs=[row, row, row] + [page_spec(pg) for pg in range(n_pages)] * 2,
            out_specs=row,
            scratch_shapes=[pltpu.VMEM((n_pages * page,) + lanes, F32)],
        ),
        out_shape=jax.ShapeDtypeStruct((n_seq,) + lanes, F32),
        compiler_params=_params("arbitrary"),
        name="attn_sample",
    )(page_table, q, k_new, v_new, *([cache_k] * n_pages), *([cache_v] * n_pages))


def _out_proj_kernel(x_ref, a_ref, c_ref, w_ref, g_ref, y_ref):
    mix = _dot(a_ref[...], w_ref[0:ATTN_WIDTH, :]) + _dot(c_ref[...], w_ref[ATTN_WIDTH:, :])
    y_ref[...] = x_ref[...] + _rms(mix, g_ref[...])


def _out_proj(x, attn, conv, w_out, g_post, *, tm):
    m, d = x.shape
    row = lambda width: pl.BlockSpec((tm, width), lambda i: (i, 0))
    return pl.pallas_call(
        _out_proj_kernel,
        grid=(m // tm,),
        in_specs=[row(d), row(ATTN_WIDTH), row(CONV_WIDTH),
                  pl.BlockSpec((d, d), lambda i: (0, 0)), pl.BlockSpec((1, d), lambda i: (0, 0))],
        out_specs=row(d),
        out_shape=jax.ShapeDtypeStruct((m, d), F32),
        compiler_params=_params("arbitrary"),
        name="out_proj",
    )(x, attn, conv, w_out, g_post)


def _ple_kernel(x_ref, hn_ref, p_ref, wg_ref, wp_ref, y_ref):
    gate = _dot(hn_ref[...], wg_ref[...])
    emb = _dot(p_ref[...].astype(BF16), wp_ref[...])
    y_ref[...] = x_ref[...] + (1.0 / (1.0 + jnp.exp(-gate))) * emb


def _ple(x, hn, p, w_gate, w_proj, *, tm, tn):
    m, d = x.shape
    pd = p.shape[1]
    return pl.pallas_call(
        _ple_kernel,
        grid=(m // tm, d // tn),
        in_specs=[pl.BlockSpec((tm, tn), lambda i, j: (i, j)),
                  pl.BlockSpec((tm, d), lambda i, j: (i, 0)),
                  pl.BlockSpec((tm, pd), lambda i, j: (i, 0)),
                  pl.BlockSpec((d, tn), lambda i, j: (0, j)),
                  pl.BlockSpec((pd, tn), lambda i, j: (0, j))],
        out_specs=pl.BlockSpec((tm, tn), lambda i, j: (i, j)),
        out_shape=jax.ShapeDtypeStruct((m, d), F32),
        compiler_params=_params("arbitrary", "arbitrary"),
        name="ple",
    )(x, hn, p, w_gate, w_proj)


def _tiles(m):
    tm = min(m, 512)
    return dict(tm=tm, tf=512, tn=ATTN_WIDTH, tp=D_MODEL)


def kernel(x_prompt, x_sample, cache_k, cache_v, state_conv, page_table, p_prompt, p_sample, w_in, w_out, w_conv, w_ffn1_gate, w_ffn1_up, w_ffn1_down, w_ffn2_gate, w_ffn2_up, w_ffn2_down, w_ple_gate, w_ple_proj, g_ffn1_pre, g_ffn1_post, g_mix_pre, g_mix_post, g_ffn2_pre, g_ffn2_post, g_ple):
    depth = w_in.shape[0]
    assert depth == 1 and x_prompt.shape[0] == 1 and x_sample.shape[1] == 1
    n_seq = x_sample.shape[0]
    seq = x_prompt.shape[1]
    n_phys, page = cache_k.shape[1], cache_k.shape[2]

    bf = lambda w: w[0].astype(BF16)
    w_in_b, w_out_b = bf(w_in), bf(w_out)
    w1g, w1u, w1d = bf(w_ffn1_gate), bf(w_ffn1_up), bf(w_ffn1_down)
    w2g, w2u, w2d = bf(w_ffn2_gate), bf(w_ffn2_up), bf(w_ffn2_down)
    w_pg, w_pp = bf(w_ple_gate), bf(w_ple_proj)
    wc = w_conv[0]

    def dense_pre(x):
        t = _tiles(x.shape[0])
        x1, h = _ffn(x, g_ffn1_pre, g_ffn1_post, g_mix_pre, w1g, w1u, w1d, tm=t["tm"], tf=t["tf"])
        q, k, v, kb = _qkv_proj(h, w_in_b, tm=t["tm"], tn=t["tn"])
        return x1, h, q, k, v, kb

    def dense_post(x1, attn, conv, p):
        t = _tiles(x1.shape[0])
        x2 = _out_proj(x1, attn, conv, w_out_b, g_mix_post, tm=t["tm"])
        x3, hn = _ffn(x2, g_ffn2_pre, g_ffn2_post, g_ple, w2g, w2u, w2d, tm=t["tm"], tf=t["tf"])
        return _ple(x3, hn, p, w_pg, w_pp, tm=t["tm"], tn=t["tp"])

    xp1, hp, qp, kp, vp, kbp = dense_pre(x_prompt[0])
    conv_p, tail_p = _conv_prompt(hp, w_in_b, wc, tm=512, tc=512)
    attn_p = _attn_prompt(qp, kbp, vp, _kmean(kp))
    y_prompt = dense_post(xp1, attn_p, conv_p, p_prompt[0, 0])

    xs1, hs, qs, ks, vs, _ = dense_pre(x_sample[:, 0, :])
    state_t = jnp.transpose(state_conv[0], (1, 0, 2))
    conv_s, nstate_t = _conv_sample(hs, w_in_b, wc, state_t, tc=512)
    heads = (ATTN_HEADS, HEAD_DIM)
    qs3, ks3, vs3 = (a.reshape(n_seq, *heads) for a in (qs, ks, vs))
    attn_s = _attn_sample(qs3, ks3, vs3, cache_k, cache_v, page_table)
    attn_s = attn_s.reshape(n_seq, ATTN_WIDTH).astype(BF16)
    y_sample = dense_post(xs1, attn_s, conv_s, p_sample[0, :, 0, :])

    return (y_prompt[None],
            y_sample[:, None, :],
            kp.reshape(1, 1, seq, *heads),
            vp.reshape(1, 1, seq, *heads),
            tail_p[V7X_SUBLANES - (CONV_K - 1):][None, None],
            ks.reshape(1, n_seq, 1, *heads),
            vs.reshape(1, n_seq, 1, *heads),
            jnp.transpose(nstate_t, (1, 0, 2))[None])
```

```python
import functools
import math

import jax
import jax.numpy as jnp
from jax import lax
from jax.experimental import pallas as pl
from jax.experimental.pallas import tpu as pltpu

F32 = jnp.float32
BF16 = jnp.bfloat16

D_MODEL = 2048
HEAD_DIM = 128
ATTN_HEADS = 8
ATTN_WIDTH = ATTN_HEADS * HEAD_DIM
CONV_WIDTH = D_MODEL - ATTN_WIDTH
CONV_K = 3
MOBA_BLOCK = 256
MOBA_TOPK = 3
RMS_EPS = 1e-6

V7X_SUBLANES = 8
V7X_VMEM_LIMIT_BYTES = 56 * 1024 * 1024

ATTN_GROUP_LOG2 = 2
ATTN_GROUP = 1 << ATTN_GROUP_LOG2
ATTN_HEADS_PER_STEP = 2
SAMPLE_CHUNK = 8

NEG_BIG = -1e30
EXP2_SCALE = (HEAD_DIM ** -0.5) * math.log2(math.e)


def _params(*semantics):
    return pltpu.CompilerParams(dimension_semantics=semantics,
                                vmem_limit_bytes=V7X_VMEM_LIMIT_BYTES)


def _rms(x, g):
    var = jnp.mean(x * x, axis=-1, keepdims=True)
    return x * lax.rsqrt(var + RMS_EPS) * g


def _dot(a, b):
    return jnp.dot(a, b, preferred_element_type=F32)


def _dot_nt(a, b):
    return lax.dot_general(a, b, (((1,), (1,)), ((), ())), preferred_element_type=F32)


def _split3(x):
    hi = x.astype(BF16)
    r1 = x - hi.astype(F32)
    mid = r1.astype(BF16)
    lo = (r1 - mid.astype(F32)).astype(BF16)
    return hi, mid, lo


def _ffn_kernel(x_ref, xs_ref, gpre_ref, gpost_ref, gnext_ref, wg_ref, wu_ref, wd_ref,
                y_ref, hn_ref, ys_ref, hns_ref, xn_ref, acc_ref, xns_ref, accs_ref):
    f = pl.program_id(1)

    def rows(x_ref, y_ref, hn_ref, xn_ref, acc_ref):
        @pl.when(f == 0)
        def _():
            xn_ref[...] = _rms(x_ref[...], gpre_ref[...]).astype(BF16)
            acc_ref[...] = jnp.zeros_like(acc_ref)

        xn = xn_ref[...]
        g = _dot(xn, wg_ref[...])
        u = _dot(xn, wu_ref[...])
        a = (g * (1.0 / (1.0 + jnp.exp(-g))) * u).astype(BF16)
        acc_ref[...] += _dot(a, wd_ref[...])

        @pl.when(f == pl.num_programs(1) - 1)
        def _():
            y = x_ref[...] + 0.5 * _rms(acc_ref[...], gpost_ref[...])
            y_ref[...] = y
            hn_ref[...] = _rms(y, gnext_ref[...]).astype(BF16)

    rows(x_ref, y_ref, hn_ref, xn_ref, acc_ref)

    @pl.when(pl.program_id(0) == pl.num_programs(0) - 1)
    def _():
        rows(xs_ref, ys_ref, hns_ref, xns_ref, accs_ref)


def _ffn(x, xs, g_pre, g_post, g_next, wg, wu, wd, *, tm, tf):
    m, d = x.shape
    ms = xs.shape[0]
    d_ff = wg.shape[1]
    row = pl.BlockSpec((tm, d), lambda i, f: (i, 0))
    small = pl.BlockSpec((ms, d), lambda i, f: (0, 0))
    vec = pl.BlockSpec((1, d), lambda i, f: (0, 0))
    return pl.pallas_call(
        _ffn_kernel,
        grid=(m // tm, d_ff // tf),
        in_specs=[row, small, vec, vec, vec,
                  pl.BlockSpec((d, tf), lambda i, f: (0, f)),
                  pl.BlockSpec((d, tf), lambda i, f: (0, f)),
                  pl.BlockSpec((tf, d), lambda i, f: (f, 0))],
        out_specs=[row, row, small, small],
        out_shape=[jax.ShapeDtypeStruct((m, d), F32), jax.ShapeDtypeStruct((m, d), BF16),
                   jax.ShapeDtypeStruct((ms, d), F32), jax.ShapeDtypeStruct((ms, d), BF16)],
        scratch_shapes=[pltpu.VMEM((tm, d), BF16), pltpu.VMEM((tm, d), F32),
                        pltpu.VMEM((ms, d), BF16), pltpu.VMEM((ms, d), F32)],
        compiler_params=_params("arbitrary", "arbitrary"),
        name="ffn",
    )(x, xs, g_pre, g_post, g_next, wg, wu, wd)


def _qkv_kernel(h_ref, wq_ref, wk_ref, wv_ref, q_ref, k_ref, v_ref, *extra):
    h = h_ref[...]
    q_ref[...] = _dot(h, wq_ref[...])
    k = _dot(h, wk_ref[...])
    k_ref[...] = k
    v_ref[...] = _dot(h, wv_ref[...])
    if extra:
        kb_ref, km_ref = extra
        kb_ref[...] = k.astype(BF16)
        blocks, width = km_ref.shape
        km_ref[...] = jnp.sum(k.reshape(blocks, MOBA_BLOCK, width), axis=1) * (1.0 / MOBA_BLOCK)


def _qkv_proj(hb, w_in, *, tm, with_block_means):
    m, d = hb.shape
    tn = ATTN_WIDTH
    wspec = lambda part: pl.BlockSpec((d, tn), lambda i: (0, part))
    ospec = pl.BlockSpec((tm, tn), lambda i: (i, 0))
    f32o = jax.ShapeDtypeStruct((m, tn), F32)
    out_specs, out_shape = [ospec] * 3, [f32o] * 3
    if with_block_means:
        bpt = tm // MOBA_BLOCK
        out_specs = out_specs + [ospec, pl.BlockSpec((None, bpt, tn), lambda i: (i, 0, 0))]
        out_shape = out_shape + [jax.ShapeDtypeStruct((m, tn), BF16),
                                 jax.ShapeDtypeStruct((m // tm, bpt, tn), F32)]
    return pl.pallas_call(
        _qkv_kernel,
        grid=(m // tm,),
        in_specs=[pl.BlockSpec((tm, d), lambda i: (i, 0)), wspec(0), wspec(1), wspec(2)],
        out_specs=out_specs,
        out_shape=out_shape,
        compiler_params=_params("arbitrary"),
        name="qkv_proj",
    )(hb, w_in, w_in, w_in)


def _conv_prompt_kernel(h_ref, wb_ref, wc_ref, wu_ref, wconv_ref, conv_ref, tail_ref, zbuf_ref):
    i = pl.program_id(1)
    tm = h_ref.shape[0]
    halo = V7X_SUBLANES

    @pl.when(i == 0)
    def _():
        zbuf_ref[0:halo, :] = jnp.zeros((halo, zbuf_ref.shape[1]), F32)

    h = h_ref[...]
    z = _dot(h, wc_ref[...]) * _dot(h, wu_ref[...])
    zbuf_ref[halo:halo + tm, :] = z
    w = wconv_ref[...]
    y = zbuf_ref[halo:halo + tm, :] * w[CONV_K - 1:CONV_K, :]
    for j in range(CONV_K - 1):
        off = halo - (CONV_K - 1) + j
        y = y + zbuf_ref[off:off + tm, :] * w[j:j + 1, :]
    conv_ref[...] = (_dot(h, wb_ref[...]) * y).astype(BF16)
    last = zbuf_ref[tm:tm + halo, :]
    zbuf_ref[0:halo, :] = last
    tail_ref[...] = last


def _conv_prompt(hb, w_in, w_conv, *, tm, tc):
    m, d = hb.shape
    nj = CONV_WIDTH // tc
    base = 3 * ATTN_WIDTH // tc
    wspec = lambda part: pl.BlockSpec((d, tc), lambda j, i: (0, base + part * nj + j))
    return pl.pallas_call(
        _conv_prompt_kernel,
        grid=(nj, m // tm),
        in_specs=[pl.BlockSpec((tm, d), lambda j, i: (i, 0)), wspec(0), wspec(1), wspec(2),
                  pl.BlockSpec((CONV_K, tc), lambda j, i: (0, j))],
        out_specs=[pl.BlockSpec((tm, tc), lambda j, i: (i, j)),
                   pl.BlockSpec((V7X_SUBLANES, tc), lambda j, i: (0, j))],
        out_shape=[jax.ShapeDtypeStruct((m, CONV_WIDTH), BF16),
                   jax.ShapeDtypeStruct((V7X_SUBLANES, CONV_WIDTH), F32)],
        scratch_shapes=[pltpu.VMEM((tm + V7X_SUBLANES, tc), F32)],
        compiler_params=_params("arbitrary", "arbitrary"),
        name="conv_prompt",
    )(hb, w_in, w_in, w_in, w_conv)


def _conv_sample_kernel(h_ref, wb_ref, wc_ref, wu_ref, wconv_ref, st_ref, conv_ref, nst_ref):
    h = h_ref[...]
    z = _dot(h, wc_ref[...]) * _dot(h, wu_ref[...])
    w = wconv_ref[...]
    y = z * w[CONV_K - 1:CONV_K, :]
    for j in range(CONV_K - 1):
        y = y + st_ref[j] * w[j:j + 1, :]
    conv_ref[...] = (_dot(h, wb_ref[...]) * y).astype(BF16)
    for j in range(CONV_K - 2):
        nst_ref[j] = st_ref[j + 1]
    nst_ref[CONV_K - 2] = z


def _conv_sample(hb, w_in, w_conv, state_t, *, tc):
    m, d = hb.shape
    nj = CONV_WIDTH // tc
    base = 3 * ATTN_WIDTH // tc
    wspec = lambda part: pl.BlockSpec((d, tc), lambda j: (0, base + part * nj + j))
    sspec = pl.BlockSpec((CONV_K - 1, m, tc), lambda j: (0, 0, j))
    return pl.pallas_call(
        _conv_sample_kernel,
        grid=(nj,),
        in_specs=[pl.BlockSpec((m, d), lambda j: (0, 0)), wspec(0), wspec(1), wspec(2),
                  pl.BlockSpec((CONV_K, tc), lambda j: (0, j)), sspec],
        out_specs=[pl.BlockSpec((m, tc), lambda j: (0, j)), sspec],
        out_shape=[jax.ShapeDtypeStruct((m, CONV_WIDTH), BF16),
                   jax.ShapeDtypeStruct((CONV_K - 1, m, CONV_WIDTH), F32)],
        compiler_params=_params("arbitrary"),
        name="conv_sample",
    )(hb, w_in, w_in, w_in, w_conv, state_t)


def _top_blocks_bias(gate, n_past, axis_len):
    blk = lax.broadcasted_iota(jnp.int32, gate.shape, 0)
    past = blk < n_past
    g = jnp.where(past, gate, -jnp.inf)
    keep = jnp.zeros(gate.shape, F32)
    for _ in range(MOBA_TOPK):
        mx = jnp.max(g, axis=0, keepdims=True)
        idx = jnp.min(jnp.where(g == mx, blk, axis_len), axis=0, keepdims=True)
        pick = blk == idx
        keep = jnp.where(pick, jnp.where(past, 1.0, keep), keep)
        g = jnp.where(pick, -jnp.inf, g)
    return jnp.where(keep > 0.0, 0.0, NEG_BIG)


def _attn_prompt_kernel(q_ref, kb_ref, v_ref, km_ref, o_ref, kx_ref, vt_ref, vtb_ref, s_ref, sd_ref):
    i = pl.program_id(1)
    n_heads, n_grp = vt_ref.shape[0], vt_ref.shape[1]
    blk = MOBA_BLOCK
    grp = ATTN_GROUP
    gk = grp * blk
    nb = n_grp * grp
    dh = HEAD_DIM

    @pl.when(i == 0)
    def _():
        col = lax.broadcasted_iota(jnp.int32, (blk, dh), 1)
        for h in range(n_heads):
            for c in range(nb):
                rows = slice(c * blk, (c + 1) * blk)
                lane0 = (c % grp) * blk
                vt = v_ref[rows, h * dh:(h + 1) * dh].T.astype(BF16)
                vt_ref[h, c // grp, :, lane0:lane0 + blk] = vt
                vtb_ref[h, c] = vt
                kx_ref[h, rows, 0:dh] = kb_ref[rows, h * dh:(h + 1) * dh]
                kx_ref[h, rows, dh:2 * dh] = jnp.where(col == c, 1.0, 0.0).astype(BF16)

    def scores(h, g, q_ext):
        kg = kx_ref[h, pl.ds(pl.multiple_of(g * gk, gk), gk), :]
        return _dot_nt(kg, q_ext)

    def update(h, g, slot, state):
        m, l, acc = state
        s = s_ref[h, slot]
        m_new = jnp.maximum(m, jnp.max(s, axis=0, keepdims=True))
        p = jnp.exp2(s - m_new)
        alpha = jnp.exp2(m - m_new)
        l = alpha * l + jnp.sum(p, axis=0, keepdims=True)
        acc = acc * alpha + _dot(vt_ref[h, g], p.astype(BF16))
        return m_new, l, acc

    qbs, gates, s0s, q_exts, state = [], [], [], [], []
    for h in range(n_heads):
        q = q_ref[:, h * dh:(h + 1) * dh]
        qbs.append((q * EXP2_SCALE).astype(BF16))
        q_hi, q_mid, _ = _split3(q)
        k_hi, k_mid, _ = _split3(km_ref[:, h * dh:(h + 1) * dh])
        gates.append(_dot_nt(k_hi, q_hi) + (_dot_nt(k_hi, q_mid) + _dot_nt(k_mid, q_hi)))

    for h in range(n_heads):
        s0s.append(_dot_nt(kb_ref[0:gk, h * dh:(h + 1) * dh], qbs[h]))
        k_own = kb_ref[pl.ds(pl.multiple_of(i * blk, blk), blk), h * dh:(h + 1) * dh]
        s = _dot_nt(k_own, qbs[h])
        key_pos = lax.broadcasted_iota(jnp.int32, s.shape, 0)
        qry_pos = lax.broadcasted_iota(jnp.int32, s.shape, 1)
        sd_ref[h] = jnp.where(key_pos <= qry_pos, s, NEG_BIG)

    for h in range(n_heads):
        bias = _top_blocks_bias(gates[h], i, nb)
        s_ref[h, 0] = (s0s[h].reshape(grp, blk, blk) + bias[0:grp][:, None, :]).reshape(gk, blk)
        bias = jnp.concatenate([bias, jnp.zeros((dh - nb, blk), F32)], axis=0)
        q_exts.append(jnp.concatenate([qbs[h], bias.T.astype(BF16)], axis=1))
        state.append((jnp.full((1, blk), NEG_BIG, F32), jnp.zeros((1, blk), F32),
                      jnp.zeros((dh, blk), F32)))

    n_groups = lax.shift_right_logical(i + (grp - 1), ATTN_GROUP_LOG2)
    n_pairs = lax.shift_right_logical(n_groups, 1)

    def body(u, carry):
        g0 = 2 * u
        for h in range(n_heads):
            s_ref[h, 1] = scores(h, g0 + 1, q_exts[h])
        carry = tuple(update(h, g0, 0, carry[h]) for h in range(n_heads))
        for h in range(n_heads):
            s_ref[h, 0] = scores(h, jnp.minimum(g0 + 2, n_grp - 1), q_exts[h])
        return tuple(update(h, g0 + 1, 1, carry[h]) for h in range(n_heads))

    state = lax.fori_loop(0, n_pairs, body, tuple(state))

    def finish(with_group):
        for h in range(n_heads):
            m, l, acc = state[h]
            sd = sd_ref[h]
            m_new = jnp.maximum(m, jnp.max(sd, axis=0, keepdims=True))
            if with_group:
                s = s_ref[h, 0]
                m_new = jnp.maximum(m_new, jnp.max(s, axis=0, keepdims=True))
            pd = jnp.exp2(sd - m_new)
            alpha = jnp.exp2(m - m_new)
            l = alpha * l + jnp.sum(pd, axis=0, keepdims=True)
            acc = acc * alpha + _dot(vtb_ref[h, i], pd.astype(BF16))
            if with_group:
                p = jnp.exp2(s - m_new)
                l = l + jnp.sum(p, axis=0, keepdims=True)
                acc = acc + _dot(vt_ref[h, 2 * n_pairs], p.astype(BF16))
            o_ref[:, h * dh:(h + 1) * dh] = (acc / l).T.astype(BF16)

    odd = 2 * n_pairs < n_groups
    pl.when(odd)(lambda: finish(True))
    pl.when(jnp.logical_not(odd))(lambda: finish(False))


def _attn_prompt(q, kb, v, km):
    t = q.shape[0]
    nb = t // MOBA_BLOCK
    hp = ATTN_HEADS_PER_STEP
    assert nb % ATTN_GROUP == 0 and ATTN_HEADS % hp == 0
    w = hp * HEAD_DIM
    return pl.pallas_call(
        _attn_prompt_kernel,
        grid=(ATTN_HEADS // hp, nb),
        in_specs=[pl.BlockSpec((MOBA_BLOCK, w), lambda h, i: (i, h)),
                  pl.BlockSpec((t, w), lambda h, i: (0, h)),
                  pl.BlockSpec((t, w), lambda h, i: (0, h)),
                  pl.BlockSpec((nb, w), lambda h, i: (0, h))],
        out_specs=pl.BlockSpec((MOBA_BLOCK, w), lambda h, i: (i, h)),
        out_shape=jax.ShapeDtypeStruct((t, ATTN_WIDTH), BF16),
        scratch_shapes=[pltpu.VMEM((hp, t, 2 * HEAD_DIM), BF16),
                        pltpu.VMEM((hp, nb // ATTN_GROUP, HEAD_DIM, ATTN_GROUP * MOBA_BLOCK), BF16),
                        pltpu.VMEM((hp, nb, HEAD_DIM, MOBA_BLOCK), BF16),
                        pltpu.VMEM((hp, 2, ATTN_GROUP * MOBA_BLOCK, MOBA_BLOCK), F32),
                        pltpu.VMEM((hp, MOBA_BLOCK, MOBA_BLOCK), F32)],
        compiler_params=_params("arbitrary", "arbitrary"),
        name="attn_prompt",
    )(q, kb, v, km)


def _attn_sample_kernel(n_pages, pt_ref, q_ref, kn_ref, vn_ref, *refs):
    k_refs = refs[:n_pages]
    v_refs = refs[n_pages:2 * n_pages]
    o_ref = refs[2 * n_pages]
    s_ref = refs[2 * n_pages + 1]
    page = k_refs[0].shape[0]
    pages_per_blk = MOBA_BLOCK // page
    n_blk = n_pages // pages_per_blk
    lanes = (ATTN_HEADS, HEAD_DIM)

    ch = SAMPLE_CHUNK
    chunks = [(pg, c * ch) for pg in range(n_pages) for c in range(page // ch)]

    q = q_ref[...]
    qe = q * EXP2_SCALE

    def score(k):
        return jnp.broadcast_to(jnp.sum(k * qe, axis=-1, keepdims=True), k.shape)

    ksum = [None] * n_blk
    smax = [None] * n_blk
    for pg, r0 in chunks:
        b = pg // pages_per_blk
        k = k_refs[pg][r0:r0 + ch]
        s = score(k)
        s_ref[pg * page + r0:pg * page + r0 + ch] = s
        ks, cm = jnp.sum(k, axis=0), jnp.max(s, axis=0)
        ksum[b] = ks if ksum[b] is None else ksum[b] + ks
        smax[b] = cm if smax[b] is None else jnp.maximum(smax[b], cm)

    kmean = jnp.concatenate([ks[None] for ks in ksum], axis=0) * (1.0 / MOBA_BLOCK)
    gate = jnp.sum(kmean * q[None], axis=-1, keepdims=True)
    bias = _top_blocks_bias(gate, n_blk, n_blk)
    s_new = score(kn_ref[...])
    m = s_new
    for b in range(n_blk):
        m = jnp.maximum(m, smax[b] + bias[b])
    offs = [m - bias[b] for b in range(n_blk)]

    p_new = jnp.exp2(s_new - m)
    l = p_new
    acc = p_new * vn_ref[...]
    for pg, r0 in chunks:
        s = s_ref[pg * page + r0:pg * page + r0 + ch]
        p = jnp.exp2(s - offs[pg // pages_per_blk])
        l = l + jnp.sum(p, axis=0)
        acc = acc + jnp.sum(p * v_refs[pg][r0:r0 + ch], axis=0)
    o_ref[...] = acc / l


def _attn_sample(q, k_new, v_new, cache_k, cache_v, page_table):
    n_seq, n_pages = page_table.shape
    page = cache_k.shape[2]
    assert MOBA_BLOCK % page == 0 and (n_pages * page) % MOBA_BLOCK == 0
    lanes = (ATTN_HEADS, HEAD_DIM)
    row = pl.BlockSpec((None,) + lanes, lambda b, pt: (b, 0, 0))
    page_spec = lambda pg: pl.BlockSpec((None, None, page) + lanes,
                                        lambda b, pt: (0, pt[b, pg], 0, 0, 0))
    return pl.pallas_call(
        functools.partial(_attn_sample_kernel, n_pages),
        grid_spec=pltpu.PrefetchScalarGridSpec(
            num_scalar_prefetch=1,
            grid=(n_seq,),
            in_specs=[row, row, row] + [page_spec(pg) for pg in range(n_pages)] * 2,
            out_specs=row,
            scratch_shapes=[pltpu.VMEM((n_pages * page,) + lanes, F32)],
        ),
        out_shape=jax.ShapeDtypeStruct((n_seq,) + lanes, F32),
        compiler_params=_params("arbitrary"),
        name="attn_sample",
    )(page_table, q, k_new, v_new, *([cache_k] * n_pages), *([cache_v] * n_pages))


def _out_proj_kernel(x_ref, a_ref, c_ref, w_ref, g_ref, y_ref):
    mix = _dot(a_ref[...], w_ref[0:ATTN_WIDTH, :]) + _dot(c_ref[...], w_ref[ATTN_WIDTH:, :])
    y_ref[...] = x_ref[...] + _rms(mix, g_ref[...])


def _out_proj(x, attn, conv, w_out, g_post, *, tm):
    m, d = x.shape
    row = lambda width: pl.BlockSpec((tm, width), lambda i: (i, 0))
    return pl.pallas_call(
        _out_proj_kernel,
        grid=(m // tm,),
        in_specs=[row(d), row(ATTN_WIDTH), row(CONV_WIDTH),
                  pl.BlockSpec((d, d), lambda i: (0, 0)), pl.BlockSpec((1, d), lambda i: (0, 0))],
        out_specs=row(d),
        out_shape=jax.ShapeDtypeStruct((m, d), F32),
        compiler_params=_params("arbitrary"),
        name="out_proj",
    )(x, attn, conv, w_out, g_post)


def _ple_kernel(x_ref, hn_ref, p_ref, wg_ref, wp_ref, y_ref):
    gate = _dot(hn_ref[...], wg_ref[...])
    emb = _dot(p_ref[...].astype(BF16), wp_ref[...])
    y_ref[...] = x_ref[...] + (1.0 / (1.0 + jnp.exp(-gate))) * emb


def _ple(x, hn, p, w_gate, w_proj, *, tm, tn):
    m, d = x.shape
    pd = p.shape[1]
    return pl.pallas_call(
        _ple_kernel,
        grid=(m // tm, d // tn),
        in_specs=[pl.BlockSpec((tm, tn), lambda i, j: (i, j)),
                  pl.BlockSpec((tm, d), lambda i, j: (i, 0)),
                  pl.BlockSpec((tm, pd), lambda i, j: (i, 0)),
                  pl.BlockSpec((d, tn), lambda i, j: (0, j)),
                  pl.BlockSpec((pd, tn), lambda i, j: (0, j))],
        out_specs=pl.BlockSpec((tm, tn), lambda i, j: (i, j)),
        out_shape=jax.ShapeDtypeStruct((m, d), F32),
        compiler_params=_params("arbitrary", "arbitrary"),
        name="ple",
    )(x, hn, p, w_gate, w_proj)


def _tiles(m):
    tm = min(m, 512)
    return dict(tm=tm, tf=512, tn=ATTN_WIDTH, tp=D_MODEL)


def kernel(x_prompt, x_sample, cache_k, cache_v, state_conv, page_table, p_prompt, p_sample, w_in, w_out, w_conv, w_ffn1_gate, w_ffn1_up, w_ffn1_down, w_ffn2_gate, w_ffn2_up, w_ffn2_down, w_ple_gate, w_ple_proj, g_ffn1_pre, g_ffn1_post, g_mix_pre, g_mix_post, g_ffn2_pre, g_ffn2_post, g_ple):
    depth = w_in.shape[0]
    assert depth == 1 and x_prompt.shape[0] == 1 and x_sample.shape[1] == 1
    n_seq = x_sample.shape[0]
    seq = x_prompt.shape[1]
    n_phys, page = cache_k.shape[1], cache_k.shape[2]

    bf = lambda w: w[0].astype(BF16)
    w_in_b, w_out_b = bf(w_in), bf(w_out)
    w1g, w1u, w1d = bf(w_ffn1_gate), bf(w_ffn1_up), bf(w_ffn1_down)
    w2g, w2u, w2d = bf(w_ffn2_gate), bf(w_ffn2_up), bf(w_ffn2_down)
    w_pg, w_pp = bf(w_ple_gate), bf(w_ple_proj)
    wc = w_conv[0]

    tp, ts = _tiles(seq), _tiles(n_seq)
    heads = (ATTN_HEADS, HEAD_DIM)

    xp1, hp, xs1, hs = _ffn(x_prompt[0], x_sample[:, 0, :], g_ffn1_pre, g_ffn1_post, g_mix_pre,
                            w1g, w1u, w1d, tm=tp["tm"], tf=tp["tf"])

    qp, kp, vp, kbp, kmp = _qkv_proj(hp, w_in_b, tm=tp["tm"], with_block_means=True)
    conv_p, tail_p = _conv_prompt(hp, w_in_b, wc, tm=512, tc=512)
    attn_p = _attn_prompt(qp, kbp, vp, kmp.reshape(seq // MOBA_BLOCK, ATTN_WIDTH))
    xp2 = _out_proj(xp1, attn_p, conv_p, w_out_b, g_mix_post, tm=tp["tm"])

    qs, ks, vs = _qkv_proj(hs, w_in_b, tm=ts["tm"], with_block_means=False)
    state_t = jnp.transpose(state_conv[0], (1, 0, 2))
    conv_s, nstate_t = _conv_sample(hs, w_in_b, wc, state_t, tc=512)
    qs3, ks3, vs3 = (a.reshape(n_seq, *heads) for a in (qs, ks, vs))
    attn_s = _attn_sample(qs3, ks3, vs3, cache_k, cache_v, page_table)
    attn_s = attn_s.reshape(n_seq, ATTN_WIDTH).astype(BF16)
    xs2 = _out_proj(xs1, attn_s, conv_s, w_out_b, g_mix_post, tm=ts["tm"])

    xp3, hnp, xs3, hns = _ffn(xp2, xs2, g_ffn2_pre, g_ffn2_post, g_ple, w2g, w2u, w2d,
                              tm=tp["tm"], tf=tp["tf"])
    y_prompt = _ple(xp3, hnp, p_prompt[0, 0], w_pg, w_pp, tm=tp["tm"], tn=tp["tp"])
    y_sample = _ple(xs3, hns, p_sample[0, :, 0, :], w_pg, w_pp, tm=ts["tm"], tn=ts["tp"])

    return (y_prompt[None],
            y_sample[:, None, :],
            kp.reshape(1, 1, seq, *heads),
            vp.reshape(1, 1, seq, *heads),
            tail_p[V7X_SUBLANES - (CONV_K - 1):][None, None],
            ks.reshape(1, n_seq, 1, *heads),
            vs.reshape(1, n_seq, 1, *heads),
            jnp.transpose(nstate_t, (1, 0, 2))[None])
```

```python
import functools
import math

import jax
import jax.numpy as jnp
from jax import lax
from jax.experimental import pallas as pl
from jax.experimental.pallas import tpu as pltpu

F32 = jnp.float32
BF16 = jnp.bfloat16

D_MODEL = 2048
HEAD_DIM = 128
ATTN_HEADS = 8
ATTN_WIDTH = ATTN_HEADS * HEAD_DIM
CONV_WIDTH = D_MODEL - ATTN_WIDTH
CONV_K = 3
MOBA_BLOCK = 256
MOBA_TOPK = 3
RMS_EPS = 1e-6

V7X_SUBLANES = 8
V7X_VMEM_LIMIT_BYTES = 56 * 1024 * 1024

ATTN_GROUP_LOG2 = 2
ATTN_GROUP = 1 << ATTN_GROUP_LOG2
ATTN_HEADS_PER_STEP = 2
SAMPLE_CHUNK = 8

NEG_BIG = -1e30
EXP2_SCALE = (HEAD_DIM ** -0.5) * math.log2(math.e)


def _params(*semantics):
    return pltpu.CompilerParams(dimension_semantics=semantics,
                                vmem_limit_bytes=V7X_VMEM_LIMIT_BYTES)


def _rms(x, g):
    var = jnp.mean(x * x, axis=-1, keepdims=True)
    return x * lax.rsqrt(var + RMS_EPS) * g


def _dot(a, b):
    return jnp.dot(a, b, preferred_element_type=F32)


def _dot_nt(a, b):
    return lax.dot_general(a, b, (((1,), (1,)), ((), ())), preferred_element_type=F32)


def _split3(x):
    hi = x.astype(BF16)
    r1 = x - hi.astype(F32)
    mid = r1.astype(BF16)
    lo = (r1 - mid.astype(F32)).astype(BF16)
    return hi, mid, lo


def _ffn_kernel(x_ref, xs_ref, gpre_ref, gpost_ref, gnext_ref, wg_ref, wu_ref, wd_ref,
                y_ref, hn_ref, ys_ref, hns_ref, xn_ref, acc_ref, xns_ref, accs_ref):
    f = pl.program_id(1)

    def rows(x_ref, y_ref, hn_ref, xn_ref, acc_ref):
        @pl.when(f == 0)
        def _():
            xn_ref[...] = _rms(x_ref[...], gpre_ref[...]).astype(BF16)
            acc_ref[...] = jnp.zeros_like(acc_ref)

        xn = xn_ref[...]
        g = _dot(xn, wg_ref[...])
        u = _dot(xn, wu_ref[...])
        a = (g * (1.0 / (1.0 + jnp.exp(-g))) * u).astype(BF16)
        acc_ref[...] += _dot(a, wd_ref[...])

        @pl.when(f == pl.num_programs(1) - 1)
        def _():
            y = x_ref[...] + 0.5 * _rms(acc_ref[...], gpost_ref[...])
            y_ref[...] = y
            hn_ref[...] = _rms(y, gnext_ref[...]).astype(BF16)

    rows(x_ref, y_ref, hn_ref, xn_ref, acc_ref)

    @pl.when(pl.program_id(0) == pl.num_programs(0) - 1)
    def _():
        rows(xs_ref, ys_ref, hns_ref, xns_ref, accs_ref)


def _ffn(x, xs, g_pre, g_post, g_next, wg, wu, wd, *, tm, tf):
    m, d = x.shape
    ms = xs.shape[0]
    d_ff = wg.shape[1]
    row = pl.BlockSpec((tm, d), lambda i, f: (i, 0))
    small = pl.BlockSpec((ms, d), lambda i, f: (0, 0))
    vec = pl.BlockSpec((1, d), lambda i, f: (0, 0))
    return pl.pallas_call(
        _ffn_kernel,
        grid=(m // tm, d_ff // tf),
        in_specs=[row, small, vec, vec, vec,
                  pl.BlockSpec((d, tf), lambda i, f: (0, f)),
                  pl.BlockSpec((d, tf), lambda i, f: (0, f)),
                  pl.BlockSpec((tf, d), lambda i, f: (f, 0))],
        out_specs=[row, row, small, small],
        out_shape=[jax.ShapeDtypeStruct((m, d), F32), jax.ShapeDtypeStruct((m, d), BF16),
                   jax.ShapeDtypeStruct((ms, d), F32), jax.ShapeDtypeStruct((ms, d), BF16)],
        scratch_shapes=[pltpu.VMEM((tm, d), BF16), pltpu.VMEM((tm, d), F32),
                        pltpu.VMEM((ms, d), BF16), pltpu.VMEM((ms, d), F32)],
        compiler_params=_params("arbitrary", "arbitrary"),
        name="ffn",
    )(x, xs, g_pre, g_post, g_next, wg, wu, wd)


def _qkv_kernel(h_ref, wq_ref, wk_ref, wv_ref, q_ref, k_ref, v_ref, *extra):
    h = h_ref[...]
    q_ref[...] = _dot(h, wq_ref[...])
    k = _dot(h, wk_ref[...])
    k_ref[...] = k
    v_ref[...] = _dot(h, wv_ref[...])
    if extra:
        kb_ref, km_ref = extra
        kb_ref[...] = k.astype(BF16)
        blocks, width = km_ref.shape
        km_ref[...] = jnp.sum(k.reshape(blocks, MOBA_BLOCK, width), axis=1) * (1.0 / MOBA_BLOCK)


def _qkv_proj(hb, w_in, *, tm, with_block_means):
    m, d = hb.shape
    tn = ATTN_WIDTH
    wspec = lambda part: pl.BlockSpec((d, tn), lambda i: (0, part))
    ospec = pl.BlockSpec((tm, tn), lambda i: (i, 0))
    f32o = jax.ShapeDtypeStruct((m, tn), F32)
    out_specs, out_shape = [ospec] * 3, [f32o] * 3
    if with_block_means:
        bpt = tm // MOBA_BLOCK
        out_specs = out_specs + [ospec, pl.BlockSpec((None, bpt, tn), lambda i: (i, 0, 0))]
        out_shape = out_shape + [jax.ShapeDtypeStruct((m, tn), BF16),
                                 jax.ShapeDtypeStruct((m // tm, bpt, tn), F32)]
    return pl.pallas_call(
        _qkv_kernel,
        grid=(m // tm,),
        in_specs=[pl.BlockSpec((tm, d), lambda i: (i, 0)), wspec(0), wspec(1), wspec(2)],
        out_specs=out_specs,
        out_shape=out_shape,
        compiler_params=_params("arbitrary"),
        name="qkv_proj",
    )(hb, w_in, w_in, w_in)


def _conv_prompt_kernel(h_ref, wb_ref, wc_ref, wu_ref, wconv_ref, conv_ref, tail_ref, zbuf_ref):
    i = pl.program_id(1)
    tm = h_ref.shape[0]
    halo = V7X_SUBLANES

    @pl.when(i == 0)
    def _():
        zbuf_ref[0:halo, :] = jnp.zeros((halo, zbuf_ref.shape[1]), F32)

    h = h_ref[...]
    z = _dot(h, wc_ref[...]) * _dot(h, wu_ref[...])
    zbuf_ref[halo:halo + tm, :] = z
    w = wconv_ref[...]
    y = zbuf_ref[halo:halo + tm, :] * w[CONV_K - 1:CONV_K, :]
    for j in range(CONV_K - 1):
        off = halo - (CONV_K - 1) + j
        y = y + zbuf_ref[off:off + tm, :] * w[j:j + 1, :]
    conv_ref[...] = (_dot(h, wb_ref[...]) * y).astype(BF16)
    last = zbuf_ref[tm:tm + halo, :]
    zbuf_ref[0:halo, :] = last
    tail_ref[...] = last


def _conv_prompt(hb, w_in, w_conv, *, tm, tc):
    m, d = hb.shape
    nj = CONV_WIDTH // tc
    base = 3 * ATTN_WIDTH // tc
    wspec = lambda part: pl.BlockSpec((d, tc), lambda j, i: (0, base + part * nj + j))
    return pl.pallas_call(
        _conv_prompt_kernel,
        grid=(nj, m // tm),
        in_specs=[pl.BlockSpec((tm, d), lambda j, i: (i, 0)), wspec(0), wspec(1), wspec(2),
                  pl.BlockSpec((CONV_K, tc), lambda j, i: (0, j))],
        out_specs=[pl.BlockSpec((tm, tc), lambda j, i: (i, j)),
                   pl.BlockSpec((V7X_SUBLANES, tc), lambda j, i: (0, j))],
        out_shape=[jax.ShapeDtypeStruct((m, CONV_WIDTH), BF16),
                   jax.ShapeDtypeStruct((V7X_SUBLANES, CONV_WIDTH), F32)],
        scratch_shapes=[pltpu.VMEM((tm + V7X_SUBLANES, tc), F32)],
        compiler_params=_params("arbitrary", "arbitrary"),
        name="conv_prompt",
    )(hb, w_in, w_in, w_in, w_conv)


def _conv_sample_kernel(h_ref, wb_ref, wc_ref, wu_ref, wconv_ref, st_ref, conv_ref, nst_ref):
    h = h_ref[...]
    z = _dot(h, wc_ref[...]) * _dot(h, wu_ref[...])
    w = wconv_ref[...]
    y = z * w[CONV_K - 1:CONV_K, :]
    for j in range(CONV_K - 1):
        y = y + st_ref[j] * w[j:j + 1, :]
    conv_ref[...] = (_dot(h, wb_ref[...]) * y).astype(BF16)
    for j in range(CONV_K - 2):
        nst_ref[j] = st_ref[j + 1]
    nst_ref[CONV_K - 2] = z


def _conv_sample(hb, w_in, w_conv, state_t, *, tc):
    m, d = hb.shape
    nj = CONV_WIDTH // tc
    base = 3 * ATTN_WIDTH // tc
    wspec = lambda part: pl.BlockSpec((d, tc), lambda j: (0, base + part * nj + j))
    sspec = pl.BlockSpec((CONV_K - 1, m, tc), lambda j: (0, 0, j))
    return pl.pallas_call(
        _conv_sample_kernel,
        grid=(nj,),
        in_specs=[pl.BlockSpec((m, d), lambda j: (0, 0)), wspec(0), wspec(1), wspec(2),
                  pl.BlockSpec((CONV_K, tc), lambda j: (0, j)), sspec],
        out_specs=[pl.BlockSpec((m, tc), lambda j: (0, j)), sspec],
        out_shape=[jax.ShapeDtypeStruct((m, CONV_WIDTH), BF16),
                   jax.ShapeDtypeStruct((CONV_K - 1, m, CONV_WIDTH), F32)],
        compiler_params=_params("arbitrary"),
        name="conv_sample",
    )(hb, w_in, w_in, w_in, w_conv, state_t)


def _top_blocks(gate, n_past, axis_len):
    blk = lax.broadcasted_iota(jnp.int32, gate.shape, 0)
    past = blk < n_past
    g = jnp.where(past, gate, -jnp.inf)
    keep = jnp.zeros(gate.shape, F32)
    chosen = []
    for _ in range(MOBA_TOPK):
        mx = jnp.max(g, axis=0, keepdims=True)
        idx = jnp.min(jnp.where(g == mx, blk, axis_len), axis=0, keepdims=True)
        pick = blk == idx
        keep = jnp.where(pick, jnp.where(past, 1.0, keep), keep)
        g = jnp.where(pick, -jnp.inf, g)
        chosen.append(idx)
    return jnp.where(keep > 0.0, 0.0, NEG_BIG), chosen


def _top_blocks_bias(gate, n_past, axis_len):
    return _top_blocks(gate, n_past, axis_len)[0]


def _attn_prompt_kernel(q_ref, kb_ref, v_ref, km_ref, o_ref, kx_ref, vt_ref, vtb_ref, s_ref, sd_ref):
    i = pl.program_id(1)
    n_heads, n_grp = vt_ref.shape[0], vt_ref.shape[1]
    blk = MOBA_BLOCK
    grp = ATTN_GROUP
    gk = grp * blk
    nb = n_grp * grp
    dh = HEAD_DIM

    @pl.when(i == 0)
    def _():
        col = lax.broadcasted_iota(jnp.int32, (blk, dh), 1)
        for h in range(n_heads):
            for c in range(nb):
                rows = slice(c * blk, (c + 1) * blk)
                lane0 = (c % grp) * blk
                vt = v_ref[rows, h * dh:(h + 1) * dh].T.astype(BF16)
                vt_ref[h, c // grp, :, lane0:lane0 + blk] = vt
                vtb_ref[h, c] = vt
                kx_ref[h, rows, 0:dh] = kb_ref[rows, h * dh:(h + 1) * dh]
                kx_ref[h, rows, dh:2 * dh] = jnp.where(col == c, 1.0, 0.0).astype(BF16)

    def scores(h, g, q_ext):
        kg = kx_ref[h, pl.ds(pl.multiple_of(g * gk, gk), gk), :]
        return _dot_nt(kg, q_ext)

    def update(h, g, slot, state):
        m, l, acc = state
        s = s_ref[h, slot]
        m_new = jnp.maximum(m, jnp.max(s, axis=0, keepdims=True))
        p = jnp.exp2(s - m_new)
        alpha = jnp.exp2(m - m_new)
        l = alpha * l + jnp.sum(p, axis=0, keepdims=True)
        acc = acc * alpha + _dot(vt_ref[h, g], p.astype(BF16))
        return m_new, l, acc

    qbs, gates, s0s, q_exts, state = [], [], [], [], []
    for h in range(n_heads):
        q = q_ref[:, h * dh:(h + 1) * dh]
        qbs.append((q * EXP2_SCALE).astype(BF16))
        q_hi, q_mid, _ = _split3(q)
        k_hi, k_mid, _ = _split3(km_ref[:, h * dh:(h + 1) * dh])
        gates.append(_dot_nt(k_hi, q_hi) + (_dot_nt(k_hi, q_mid) + _dot_nt(k_mid, q_hi)))

    for h in range(n_heads):
        s0s.append(_dot_nt(kb_ref[0:gk, h * dh:(h + 1) * dh], qbs[h]))
        k_own = kb_ref[pl.ds(pl.multiple_of(i * blk, blk), blk), h * dh:(h + 1) * dh]
        s = _dot_nt(k_own, qbs[h])
        key_pos = lax.broadcasted_iota(jnp.int32, s.shape, 0)
        qry_pos = lax.broadcasted_iota(jnp.int32, s.shape, 1)
        sd_ref[h] = jnp.where(key_pos <= qry_pos, s, NEG_BIG)

    for h in range(n_heads):
        bias = _top_blocks_bias(gates[h], i, nb)
        s_ref[h, 0] = (s0s[h].reshape(grp, blk, blk) + bias[0:grp][:, None, :]).reshape(gk, blk)
        bias = jnp.concatenate([bias, jnp.zeros((dh - nb, blk), F32)], axis=0)
        q_exts.append(jnp.concatenate([qbs[h], bias.T.astype(BF16)], axis=1))
        state.append((jnp.full((1, blk), NEG_BIG, F32), jnp.zeros((1, blk), F32),
                      jnp.zeros((dh, blk), F32)))

    n_groups = lax.shift_right_logical(i + (grp - 1), ATTN_GROUP_LOG2)
    n_pairs = lax.shift_right_logical(n_groups, 1)

    def body(u, carry):
        g0 = 2 * u
        for h in range(n_heads):
            s_ref[h, 1] = scores(h, g0 + 1, q_exts[h])
        carry = tuple(update(h, g0, 0, carry[h]) for h in range(n_heads))
        for h in range(n_heads):
            s_ref[h, 0] = scores(h, jnp.minimum(g0 + 2, n_grp - 1), q_exts[h])
        return tuple(update(h, g0 + 1, 1, carry[h]) for h in range(n_heads))

    state = lax.fori_loop(0, n_pairs, body, tuple(state))

    def finish(with_group):
        for h in range(n_heads):
            m, l, acc = state[h]
            sd = sd_ref[h]
            m_new = jnp.maximum(m, jnp.max(sd, axis=0, keepdims=True))
            if with_group:
                s = s_ref[h, 0]
                m_new = jnp.maximum(m_new, jnp.max(s, axis=0, keepdims=True))
            pd = jnp.exp2(sd - m_new)
            alpha = jnp.exp2(m - m_new)
            l = alpha * l + jnp.sum(pd, axis=0, keepdims=True)
            acc = acc * alpha + _dot(vtb_ref[h, i], pd.astype(BF16))
            if with_group:
                p = jnp.exp2(s - m_new)
                l = l + jnp.sum(p, axis=0, keepdims=True)
                acc = acc + _dot(vt_ref[h, 2 * n_pairs], p.astype(BF16))
            o_ref[:, h * dh:(h + 1) * dh] = (acc / l).T.astype(BF16)

    odd = 2 * n_pairs < n_groups
    pl.when(odd)(lambda: finish(True))
    pl.when(jnp.logical_not(odd))(lambda: finish(False))


def _attn_prompt(q, kb, v, km):
    t = q.shape[0]
    nb = t // MOBA_BLOCK
    hp = ATTN_HEADS_PER_STEP
    assert nb % ATTN_GROUP == 0 and ATTN_HEADS % hp == 0
    w = hp * HEAD_DIM
    return pl.pallas_call(
        _attn_prompt_kernel,
        grid=(ATTN_HEADS // hp, nb),
        in_specs=[pl.BlockSpec((MOBA_BLOCK, w), lambda h, i: (i, h)),
                  pl.BlockSpec((t, w), lambda h, i: (0, h)),
                  pl.BlockSpec((t, w), lambda h, i: (0, h)),
                  pl.BlockSpec((nb, w), lambda h, i: (0, h))],
        out_specs=pl.BlockSpec((MOBA_BLOCK, w), lambda h, i: (i, h)),
        out_shape=jax.ShapeDtypeStruct((t, ATTN_WIDTH), BF16),
        scratch_shapes=[pltpu.VMEM((hp, t, 2 * HEAD_DIM), BF16),
                        pltpu.VMEM((hp, nb // ATTN_GROUP, HEAD_DIM, ATTN_GROUP * MOBA_BLOCK), BF16),
                        pltpu.VMEM((hp, nb, HEAD_DIM, MOBA_BLOCK), BF16),
                        pltpu.VMEM((hp, 2, ATTN_GROUP * MOBA_BLOCK, MOBA_BLOCK), F32),
                        pltpu.VMEM((hp, MOBA_BLOCK, MOBA_BLOCK), F32)],
        compiler_params=_params("arbitrary", "arbitrary"),
        name="attn_prompt",
    )(q, kb, v, km)


def _attn_sample_kernel(n_pages, pt_ref, q_ref, kn_ref, vn_ref, *refs):
    k_refs = refs[:n_pages]
    cv_ref, o_ref, s_ref, vbuf_ref, sem = refs[n_pages:]
    seq = pl.program_id(0)
    page = k_refs[0].shape[0]
    pages_per_blk = MOBA_BLOCK // page
    n_blk = n_pages // pages_per_blk
    n_heads, sub = ATTN_HEADS, V7X_SUBLANES

    ch = SAMPLE_CHUNK
    chunks = [(pg, c * ch) for pg in range(n_pages) for c in range(page // ch)]

    q = q_ref[...]
    qe = q * EXP2_SCALE

    def score(k):
        return jnp.broadcast_to(jnp.sum(k * qe, axis=-1, keepdims=True), k.shape)

    ksum = [None] * n_blk
    for pg, r0 in chunks:
        b = pg // pages_per_blk
        ks = jnp.sum(k_refs[pg][r0:r0 + ch], axis=0)
        ksum[b] = ks if ksum[b] is None else ksum[b] + ks
    kmean = jnp.concatenate([ks[None] for ks in ksum], axis=0) * (1.0 / MOBA_BLOCK)
    gate = jnp.sum(kmean * q[None], axis=-1, keepdims=True)
    bias, chosen = _top_blocks(gate, n_blk, n_blk)
    picks = [(h, r, jnp.minimum(chosen[r][0, h, 0], n_blk - 1))
             for h in range(n_heads) for r in range(MOBA_TOPK)]

    def slab_copy(h, r, blk_idx, part):
        page_id = pt_ref[seq, blk_idx * pages_per_blk + part]
        return pltpu.make_async_copy(cv_ref.at[0, page_id, :, h, :],
                                     vbuf_ref.at[h, r, pl.ds(part * page, page), :], sem)

    for h, r, blk_idx in picks:
        for part in range(pages_per_blk):
            slab_copy(h, r, blk_idx, part).start()

    smax = [None] * n_blk
    for pg, r0 in chunks:
        b = pg // pages_per_blk
        s = score(k_refs[pg][r0:r0 + ch])
        row0 = (pg * page + r0) * sub
        s_ref[row0:row0 + ch * sub, :] = s.reshape(ch * sub, HEAD_DIM)
        cm = jnp.max(s, axis=0)
        smax[b] = cm if smax[b] is None else jnp.maximum(smax[b], cm)
    s_new = score(kn_ref[...])
    m = s_new
    for b in range(n_blk):
        m = jnp.maximum(m, smax[b] + bias[b])

    for h, r, blk_idx in picks:
        for part in range(pages_per_blk):
            slab_copy(h, r, blk_idx, part).wait()

    fold = lambda x: jnp.sum(x.reshape(MOBA_BLOCK // sub, sub, HEAD_DIM), axis=0)
    for h in range(n_heads):
        m_h = m[h:h + 1, :]
        p_new = jnp.exp2(s_new[h:h + 1, :] - m_h)
        l_h = jnp.zeros((sub, HEAD_DIM), F32)
        acc_h = jnp.zeros((sub, HEAD_DIM), F32)
        for hh, r, blk_idx in picks:
            if hh != h:
                continue
            rows = pl.ds(blk_idx * (MOBA_BLOCK * sub) + h, MOBA_BLOCK, stride=sub)
            p = jnp.exp2(s_ref[rows, :] - m_h)
            l_h = l_h + fold(p)
            acc_h = acc_h + fold(p * vbuf_ref[h, r])
        l_row = jnp.sum(l_h, axis=0, keepdims=True) + p_new
        acc_row = jnp.sum(acc_h, axis=0, keepdims=True) + p_new * vn_ref[h:h + 1, :]
        o_ref[h:h + 1, :] = acc_row / l_row


def _attn_sample(q, k_new, v_new, cache_k, cache_v, page_table):
    n_seq, n_pages = page_table.shape
    page = cache_k.shape[2]
    assert MOBA_BLOCK % page == 0 and (n_pages * page) % MOBA_BLOCK == 0
    assert n_pages * page // MOBA_BLOCK >= MOBA_TOPK and ATTN_HEADS == V7X_SUBLANES
    lanes = (ATTN_HEADS, HEAD_DIM)
    row = pl.BlockSpec((None,) + lanes, lambda b, pt: (b, 0, 0))
    page_spec = lambda pg: pl.BlockSpec((None, None, page) + lanes,
                                        lambda b, pt: (0, pt[b, pg], 0, 0, 0))
    return pl.pallas_call(
        functools.partial(_attn_sample_kernel, n_pages),
        grid_spec=pltpu.PrefetchScalarGridSpec(
            num_scalar_prefetch=1,
            grid=(n_seq,),
            in_specs=([row, row, row] + [page_spec(pg) for pg in range(n_pages)]
                      + [pl.BlockSpec(memory_space=pl.ANY)]),
            out_specs=row,
            scratch_shapes=[pltpu.VMEM((n_pages * page * ATTN_HEADS, HEAD_DIM), F32),
                            pltpu.VMEM((ATTN_HEADS, MOBA_TOPK, MOBA_BLOCK, HEAD_DIM), F32),
                            pltpu.SemaphoreType.DMA(())],
        ),
        out_shape=jax.ShapeDtypeStruct((n_seq,) + lanes, F32),
        compiler_params=_params("arbitrary"),
        name="attn_sample",
    )(page_table, q, k_new, v_new, *([cache_k] * n_pages), cache_v)


def _out_proj_kernel(x_ref, a_ref, c_ref, w_ref, g_ref, y_ref):
    mix = _dot(a_ref[...], w_ref[0:ATTN_WIDTH, :]) + _dot(c_ref[...], w_ref[ATTN_WIDTH:, :])
    y_ref[...] = x_ref[...] + _rms(mix, g_ref[...])


def _out_proj(x, attn, conv, w_out, g_post, *, tm):
    m, d = x.shape
    row = lambda width: pl.BlockSpec((tm, width), lambda i: (i, 0))
    return pl.pallas_call(
        _out_proj_kernel,
        grid=(m // tm,),
        in_specs=[row(d), row(ATTN_WIDTH), row(CONV_WIDTH),
                  pl.BlockSpec((d, d), lambda i: (0, 0)), pl.BlockSpec((1, d), lambda i: (0, 0))],
        out_specs=row(d),
        out_shape=jax.ShapeDtypeStruct((m, d), F32),
        compiler_params=_params("arbitrary"),
        name="out_proj",
    )(x, attn, conv, w_out, g_post)


def _ple_kernel(x_ref, hn_ref, p_ref, wg_ref, wp_ref, y_ref):
    gate = _dot(hn_ref[...], wg_ref[...])
    emb = _dot(p_ref[...].astype(BF16), wp_ref[...])
    y_ref[...] = x_ref[...] + (1.0 / (1.0 + jnp.exp(-gate))) * emb


def _ple(x, hn, p, w_gate, w_proj, *, tm, tn):
    m, d = x.shape
    pd = p.shape[1]
    return pl.pallas_call(
        _ple_kernel,
        grid=(m // tm, d // tn),
        in_specs=[pl.BlockSpec((tm, tn), lambda i, j: (i, j)),
                  pl.BlockSpec((tm, d), lambda i, j: (i, 0)),
                  pl.BlockSpec((tm, pd), lambda i, j: (i, 0)),
                  pl.BlockSpec((d, tn), lambda i, j: (0, j)),
                  pl.BlockSpec((pd, tn), lambda i, j: (0, j))],
        out_specs=pl.BlockSpec((tm, tn), lambda i, j: (i, j)),
        out_shape=jax.ShapeDtypeStruct((m, d), F32),
        compiler_params=_params("arbitrary", "arbitrary"),
        name="ple",
    )(x, hn, p, w_gate, w_proj)


def _tiles(m):
    tm = min(m, 512)
    return dict(tm=tm, tf=512, tn=ATTN_WIDTH, tp=D_MODEL)


def kernel(x_prompt, x_sample, cache_k, cache_v, state_conv, page_table, p_prompt, p_sample, w_in, w_out, w_conv, w_ffn1_gate, w_ffn1_up, w_ffn1_down, w_ffn2_gate, w_ffn2_up, w_ffn2_down, w_ple_gate, w_ple_proj, g_ffn1_pre, g_ffn1_post, g_mix_pre, g_mix_post, g_ffn2_pre, g_ffn2_post, g_ple):
    depth = w_in.shape[0]
    assert depth == 1 and x_prompt.shape[0] == 1 and x_sample.shape[1] == 1
    n_seq = x_sample.shape[0]
    seq = x_prompt.shape[1]
    n_phys, page = cache_k.shape[1], cache_k.shape[2]

    bf = lambda w: w[0].astype(BF16)
    w_in_b, w_out_b = bf(w_in), bf(w_out)
    w1g, w1u, w1d = bf(w_ffn1_gate), bf(w_ffn1_up), bf(w_ffn1_down)
    w2g, w2u, w2d = bf(w_ffn2_gate), bf(w_ffn2_up), bf(w_ffn2_down)
    w_pg, w_pp = bf(w_ple_gate), bf(w_ple_proj)
    wc = w_conv[0]

    tp, ts = _tiles(seq), _tiles(n_seq)
    heads = (ATTN_HEADS, HEAD_DIM)

    xp1, hp, xs1, hs = _ffn(x_prompt[0], x_sample[:, 0, :], g_ffn1_pre, g_ffn1_post, g_mix_pre,
                            w1g, w1u, w1d, tm=tp["tm"], tf=tp["tf"])

    qp, kp, vp, kbp, kmp = _qkv_proj(hp, w_in_b, tm=tp["tm"], with_block_means=True)
    conv_p, tail_p = _conv_prompt(hp, w_in_b, wc, tm=512, tc=512)
    attn_p = _attn_prompt(qp, kbp, vp, kmp.reshape(seq // MOBA_BLOCK, ATTN_WIDTH))
    xp2 = _out_proj(xp1, attn_p, conv_p, w_out_b, g_mix_post, tm=tp["tm"])

    qs, ks, vs = _qkv_proj(hs, w_in_b, tm=ts["tm"], with_block_means=False)
    state_t = jnp.transpose(state_conv[0], (1, 0, 2))
    conv_s, nstate_t = _conv_sample(hs, w_in_b, wc, state_t, tc=512)
    qs3, ks3, vs3 = (a.reshape(n_seq, *heads) for a in (qs, ks, vs))
    attn_s = _attn_sample(qs3, ks3, vs3, cache_k, cache_v, page_table)
    attn_s = attn_s.reshape(n_seq, ATTN_WIDTH).astype(BF16)
    xs2 = _out_proj(xs1, attn_s, conv_s, w_out_b, g_mix_post, tm=ts["tm"])

    xp3, hnp, xs3, hns = _ffn(xp2, xs2, g_ffn2_pre, g_ffn2_post, g_ple, w2g, w2u, w2d,
                              tm=tp["tm"], tf=tp["tf"])
    y_prompt = _ple(xp3, hnp, p_prompt[0, 0], w_pg, w_pp, tm=tp["tm"], tn=tp["tp"])
    y_sample = _ple(xs3, hns, p_sample[0, :, 0, :], w_pg, w_pp, tm=ts["tm"], tn=ts["tp"])

    return (y_prompt[None],
            y_sample[:, None, :],
            kp.reshape(1, 1, seq, *heads),
            vp.reshape(1, 1, seq, *heads),
            tail_p[V7X_SUBLANES - (CONV_K - 1):][None, None],
            ks.reshape(1, n_seq, 1, *heads),
            vs.reshape(1, n_seq, 1, *heads),
            jnp.transpose(nstate_t, (1, 0, 2))[None])
```

```python
import functools
import math

import jax
import jax.numpy as jnp
from jax import lax
from jax.experimental import pallas as pl
from jax.experimental.pallas import tpu as pltpu

F32 = jnp.float32
BF16 = jnp.bfloat16

D_MODEL = 2048
HEAD_DIM = 128
ATTN_HEADS = 8
ATTN_WIDTH = ATTN_HEADS * HEAD_DIM
CONV_WIDTH = D_MODEL - ATTN_WIDTH
CONV_K = 3
MOBA_BLOCK = 256
MOBA_TOPK = 3
RMS_EPS = 1e-6

V7X_SUBLANES = 8
V7X_VMEM_LIMIT_BYTES = 56 * 1024 * 1024

ATTN_GROUP_LOG2 = 2
ATTN_GROUP = 1 << ATTN_GROUP_LOG2
ATTN_HEADS_PER_STEP = 2
SAMPLE_CHUNK = 8

NEG_BIG = -1e30
EXP2_SCALE = (HEAD_DIM ** -0.5) * math.log2(math.e)


def _params(*semantics):
    return pltpu.CompilerParams(dimension_semantics=semantics,
                                vmem_limit_bytes=V7X_VMEM_LIMIT_BYTES)


def _rms(x, g):
    var = jnp.mean(x * x, axis=-1, keepdims=True)
    return x * lax.rsqrt(var + RMS_EPS) * g


def _dot(a, b):
    return jnp.dot(a, b, preferred_element_type=F32)


def _dot_nt(a, b):
    return lax.dot_general(a, b, (((1,), (1,)), ((), ())), preferred_element_type=F32)


def _split3(x):
    hi = x.astype(BF16)
    r1 = x - hi.astype(F32)
    mid = r1.astype(BF16)
    lo = (r1 - mid.astype(F32)).astype(BF16)
    return hi, mid, lo


def _ffn_kernel(x_ref, xs_ref, gpre_ref, gpost_ref, gnext_ref, wg_ref, wu_ref, wd_ref,
                y_ref, hn_ref, ys_ref, hns_ref, xn_ref, acc_ref, xns_ref, accs_ref):
    f = pl.program_id(1)

    def rows(x_ref, y_ref, hn_ref, xn_ref, acc_ref):
        @pl.when(f == 0)
        def _():
            xn_ref[...] = _rms(x_ref[...], gpre_ref[...]).astype(BF16)
            acc_ref[...] = jnp.zeros_like(acc_ref)

        xn = xn_ref[...]
        g = _dot(xn, wg_ref[...])
        u = _dot(xn, wu_ref[...])
        a = (g * (1.0 / (1.0 + jnp.exp(-g))) * u).astype(BF16)
        acc_ref[...] += _dot(a, wd_ref[...])

        @pl.when(f == pl.num_programs(1) - 1)
        def _():
            y = x_ref[...] + 0.5 * _rms(acc_ref[...], gpost_ref[...])
            y_ref[...] = y
            hn_ref[...] = _rms(y, gnext_ref[...]).astype(BF16)

    rows(x_ref, y_ref, hn_ref, xn_ref, acc_ref)

    @pl.when(pl.program_id(0) == pl.num_programs(0) - 1)
    def _():
        rows(xs_ref, ys_ref, hns_ref, xns_ref, accs_ref)


def _ffn(x, xs, g_pre, g_post, g_next, wg, wu, wd, *, tm, tf):
    m, d = x.shape
    ms = xs.shape[0]
    d_ff = wg.shape[1]
    row = pl.BlockSpec((tm, d), lambda i, f: (i, 0))
    small = pl.BlockSpec((ms, d), lambda i, f: (0, 0))
    vec = pl.BlockSpec((1, d), lambda i, f: (0, 0))
    return pl.pallas_call(
        _ffn_kernel,
        grid=(m // tm, d_ff // tf),
        in_specs=[row, small, vec, vec, vec,
                  pl.BlockSpec((d, tf), lambda i, f: (0, f)),
                  pl.BlockSpec((d, tf), lambda i, f: (0, f)),
                  pl.BlockSpec((tf, d), lambda i, f: (f, 0))],
        out_specs=[row, row, small, small],
        out_shape=[jax.ShapeDtypeStruct((m, d), F32), jax.ShapeDtypeStruct((m, d), BF16),
                   jax.ShapeDtypeStruct((ms, d), F32), jax.ShapeDtypeStruct((ms, d), BF16)],
        scratch_shapes=[pltpu.VMEM((tm, d), BF16), pltpu.VMEM((tm, d), F32),
                        pltpu.VMEM((ms, d), BF16), pltpu.VMEM((ms, d), F32)],
        compiler_params=_params("arbitrary", "arbitrary"),
        name="ffn",
    )(x, xs, g_pre, g_post, g_next, wg, wu, wd)


def _qkv_kernel(h_ref, wq_ref, wk_ref, wv_ref, q_ref, k_ref, v_ref, *extra):
    h = h_ref[...]
    q_ref[...] = _dot(h, wq_ref[...])
    k = _dot(h, wk_ref[...])
    k_ref[...] = k
    v_ref[...] = _dot(h, wv_ref[...])
    if extra:
        kb_ref, km_ref = extra
        kb_ref[...] = k.astype(BF16)
        blocks, width = km_ref.shape
        km_ref[...] = jnp.sum(k.reshape(blocks, MOBA_BLOCK, width), axis=1) * (1.0 / MOBA_BLOCK)


def _qkv_proj(hb, w_in, *, tm, with_block_means):
    m, d = hb.shape
    tn = ATTN_WIDTH
    wspec = lambda part: pl.BlockSpec((d, tn), lambda i: (0, part))
    ospec = pl.BlockSpec((tm, tn), lambda i: (i, 0))
    f32o = jax.ShapeDtypeStruct((m, tn), F32)
    out_specs, out_shape = [ospec] * 3, [f32o] * 3
    if with_block_means:
        bpt = tm // MOBA_BLOCK
        out_specs = out_specs + [ospec, pl.BlockSpec((None, bpt, tn), lambda i: (i, 0, 0))]
        out_shape = out_shape + [jax.ShapeDtypeStruct((m, tn), BF16),
                                 jax.ShapeDtypeStruct((m // tm, bpt, tn), F32)]
    return pl.pallas_call(
        _qkv_kernel,
        grid=(m // tm,),
        in_specs=[pl.BlockSpec((tm, d), lambda i: (i, 0)), wspec(0), wspec(1), wspec(2)],
        out_specs=out_specs,
        out_shape=out_shape,
        compiler_params=_params("arbitrary"),
        name="qkv_proj",
    )(hb, w_in, w_in, w_in)


def _conv_prompt_kernel(h_ref, wb_ref, wc_ref, wu_ref, wconv_ref, conv_ref, tail_ref, zbuf_ref):
    i = pl.program_id(1)
    tm = h_ref.shape[0]
    halo = V7X_SUBLANES

    @pl.when(i == 0)
    def _():
        zbuf_ref[0:halo, :] = jnp.zeros((halo, zbuf_ref.shape[1]), F32)

    h = h_ref[...]
    z = _dot(h, wc_ref[...]) * _dot(h, wu_ref[...])
    zbuf_ref[halo:halo + tm, :] = z
    w = wconv_ref[...]
    y = zbuf_ref[halo:halo + tm, :] * w[CONV_K - 1:CONV_K, :]
    for j in range(CONV_K - 1):
        off = halo - (CONV_K - 1) + j
        y = y + zbuf_ref[off:off + tm, :] * w[j:j + 1, :]
    conv_ref[...] = (_dot(h, wb_ref[...]) * y).astype(BF16)
    last = zbuf_ref[tm:tm + halo, :]
    zbuf_ref[0:halo, :] = last
    tail_ref[...] = last


def _conv_prompt(hb, w_in, w_conv, *, tm, tc):
    m, d = hb.shape
    nj = CONV_WIDTH // tc
    base = 3 * ATTN_WIDTH // tc
    wspec = lambda part: pl.BlockSpec((d, tc), lambda j, i: (0, base + part * nj + j))
    return pl.pallas_call(
        _conv_prompt_kernel,
        grid=(nj, m // tm),
        in_specs=[pl.BlockSpec((tm, d), lambda j, i: (i, 0)), wspec(0), wspec(1), wspec(2),
                  pl.BlockSpec((CONV_K, tc), lambda j, i: (0, j))],
        out_specs=[pl.BlockSpec((tm, tc), lambda j, i: (i, j)),
                   pl.BlockSpec((V7X_SUBLANES, tc), lambda j, i: (0, j))],
        out_shape=[jax.ShapeDtypeStruct((m, CONV_WIDTH), BF16),
                   jax.ShapeDtypeStruct((V7X_SUBLANES, CONV_WIDTH), F32)],
        scratch_shapes=[pltpu.VMEM((tm + V7X_SUBLANES, tc), F32)],
        compiler_params=_params("arbitrary", "arbitrary"),
        name="conv_prompt",
    )(hb, w_in, w_in, w_in, w_conv)


def _conv_sample_kernel(h_ref, wb_ref, wc_ref, wu_ref, wconv_ref, st_ref, conv_ref, nst_ref):
    h = h_ref[...]
    z = _dot(h, wc_ref[...]) * _dot(h, wu_ref[...])
    w = wconv_ref[...]
    y = z * w[CONV_K - 1:CONV_K, :]
    for j in range(CONV_K - 1):
        y = y + st_ref[j] * w[j:j + 1, :]
    conv_ref[...] = (_dot(h, wb_ref[...]) * y).astype(BF16)
    for j in range(CONV_K - 2):
        nst_ref[j] = st_ref[j + 1]
    nst_ref[CONV_K - 2] = z


def _conv_sample(hb, w_in, w_conv, state_t, *, tc):
    m, d = hb.shape
    nj = CONV_WIDTH // tc
    base = 3 * ATTN_WIDTH // tc
    wspec = lambda part: pl.BlockSpec((d, tc), lambda j: (0, base + part * nj + j))
    sspec = pl.BlockSpec((CONV_K - 1, m, tc), lambda j: (0, 0, j))
    return pl.pallas_call(
        _conv_sample_kernel,
        grid=(nj,),
        in_specs=[pl.BlockSpec((m, d), lambda j: (0, 0)), wspec(0), wspec(1), wspec(2),
                  pl.BlockSpec((CONV_K, tc), lambda j: (0, j)), sspec],
        out_specs=[pl.BlockSpec((m, tc), lambda j: (0, j)), sspec],
        out_shape=[jax.ShapeDtypeStruct((m, CONV_WIDTH), BF16),
                   jax.ShapeDtypeStruct((CONV_K - 1, m, CONV_WIDTH), F32)],
        compiler_params=_params("arbitrary"),
        name="conv_sample",
    )(hb, w_in, w_in, w_in, w_conv, state_t)


def _top_blocks(gate, n_past, axis_len):
    blk = lax.broadcasted_iota(jnp.int32, gate.shape, 0)
    past = blk < n_past
    g = jnp.where(past, gate, -jnp.inf)
    keep = jnp.zeros(gate.shape, F32)
    chosen = []
    for _ in range(MOBA_TOPK):
        mx = jnp.max(g, axis=0, keepdims=True)
        idx = jnp.min(jnp.where(g == mx, blk, axis_len), axis=0, keepdims=True)
        pick = blk == idx
        keep = jnp.where(pick, jnp.where(past, 1.0, keep), keep)
        g = jnp.where(pick, -jnp.inf, g)
        chosen.append(idx)
    return jnp.where(keep > 0.0, 0.0, NEG_BIG), chosen


def _top_blocks_bias(gate, n_past, axis_len):
    return _top_blocks(gate, n_past, axis_len)[0]


def _attn_prompt_kernel(q_ref, kb_ref, v_ref, km_ref, o_ref, kx_ref, vt_ref, vtb_ref, s_ref, sd_ref):
    i = pl.program_id(1)
    n_heads, n_grp = vt_ref.shape[0], vt_ref.shape[1]
    blk = MOBA_BLOCK
    grp = ATTN_GROUP
    gk = grp * blk
    nb = n_grp * grp
    dh = HEAD_DIM

    @pl.when(i == 0)
    def _():
        col = lax.broadcasted_iota(jnp.int32, (blk, dh), 1)
        for h in range(n_heads):
            for c in range(nb):
                rows = slice(c * blk, (c + 1) * blk)
                lane0 = (c % grp) * blk
                vt = v_ref[rows, h * dh:(h + 1) * dh].T.astype(BF16)
                vt_ref[h, c // grp, :, lane0:lane0 + blk] = vt
                vtb_ref[h, c] = vt
                kx_ref[h, rows, 0:dh] = kb_ref[rows, h * dh:(h + 1) * dh]
                kx_ref[h, rows, dh:2 * dh] = jnp.where(col == c, 1.0, 0.0).astype(BF16)

    def scores(h, g, q_ext):
        kg = kx_ref[h, pl.ds(pl.multiple_of(g * gk, gk), gk), :]
        return _dot_nt(kg, q_ext)

    def update(h, g, slot, state):
        m, l, acc = state
        s = s_ref[h, slot]
        m_new = jnp.maximum(m, jnp.max(s, axis=0, keepdims=True))
        p = jnp.exp2(s - m_new)
        alpha = jnp.exp2(m - m_new)
        l = alpha * l + jnp.sum(p, axis=0, keepdims=True)
        acc = acc * alpha + _dot(vt_ref[h, g], p.astype(BF16))
        return m_new, l, acc

    qbs, gates, s0s, q_exts, state = [], [], [], [], []
    for h in range(n_heads):
        q = q_ref[:, h * dh:(h + 1) * dh]
        qbs.append((q * EXP2_SCALE).astype(BF16))
        q_hi, q_mid, _ = _split3(q)
        k_hi, k_mid, _ = _split3(km_ref[:, h * dh:(h + 1) * dh])
        gates.append(_dot_nt(k_hi, q_hi) + (_dot_nt(k_hi, q_mid) + _dot_nt(k_mid, q_hi)))

    for h in range(n_heads):
        s0s.append(_dot_nt(kb_ref[0:gk, h * dh:(h + 1) * dh], qbs[h]))
        k_own = kb_ref[pl.ds(pl.multiple_of(i * blk, blk), blk), h * dh:(h + 1) * dh]
        s = _dot_nt(k_own, qbs[h])
        key_pos = lax.broadcasted_iota(jnp.int32, s.shape, 0)
        qry_pos = lax.broadcasted_iota(jnp.int32, s.shape, 1)
        sd_ref[h] = jnp.where(key_pos <= qry_pos, s, NEG_BIG)

    for h in range(n_heads):
        bias = _top_blocks_bias(gates[h], i, nb)
        s_ref[h, 0] = (s0s[h].reshape(grp, blk, blk) + bias[0:grp][:, None, :]).reshape(gk, blk)
        bias = jnp.concatenate([bias, jnp.zeros((dh - nb, blk), F32)], axis=0)
        q_exts.append(jnp.concatenate([qbs[h], bias.T.astype(BF16)], axis=1))
        state.append((jnp.full((1, blk), NEG_BIG, F32), jnp.zeros((1, blk), F32),
                      jnp.zeros((dh, blk), F32)))

    n_groups = lax.shift_right_logical(i + (grp - 1), ATTN_GROUP_LOG2)
    n_pairs = lax.shift_right_logical(n_groups, 1)

    def body(u, carry):
        g0 = 2 * u
        for h in range(n_heads):
            s_ref[h, 1] = scores(h, g0 + 1, q_exts[h])
        carry = tuple(update(h, g0, 0, carry[h]) for h in range(n_heads))
        for h in range(n_heads):
            s_ref[h, 0] = scores(h, jnp.minimum(g0 + 2, n_grp - 1), q_exts[h])
        return tuple(update(h, g0 + 1, 1, carry[h]) for h in range(n_heads))

    state = lax.fori_loop(0, n_pairs, body, tuple(state))

    def finish(with_group):
        for h in range(n_heads):
            m, l, acc = state[h]
            sd = sd_ref[h]
            m_new = jnp.maximum(m, jnp.max(sd, axis=0, keepdims=True))
            if with_group:
                s = s_ref[h, 0]
                m_new = jnp.maximum(m_new, jnp.max(s, axis=0, keepdims=True))
            pd = jnp.exp2(sd - m_new)
            alpha = jnp.exp2(m - m_new)
            l = alpha * l + jnp.sum(pd, axis=0, keepdims=True)
            acc = acc * alpha + _dot(vtb_ref[h, i], pd.astype(BF16))
            if with_group:
                p = jnp.exp2(s - m_new)
                l = l + jnp.sum(p, axis=0, keepdims=True)
                acc = acc + _dot(vt_ref[h, 2 * n_pairs], p.astype(BF16))
            o_ref[:, h * dh:(h + 1) * dh] = (acc / l).T.astype(BF16)

    odd = 2 * n_pairs < n_groups
    pl.when(odd)(lambda: finish(True))
    pl.when(jnp.logical_not(odd))(lambda: finish(False))


def _attn_prompt(q, kb, v, km):
    t = q.shape[0]
    nb = t // MOBA_BLOCK
    hp = ATTN_HEADS_PER_STEP
    assert nb % ATTN_GROUP == 0 and ATTN_HEADS % hp == 0
    w = hp * HEAD_DIM
    return pl.pallas_call(
        _attn_prompt_kernel,
        grid=(ATTN_HEADS // hp, nb),
        in_specs=[pl.BlockSpec((MOBA_BLOCK, w), lambda h, i: (i, h)),
                  pl.BlockSpec((t, w), lambda h, i: (0, h)),
                  pl.BlockSpec((t, w), lambda h, i: (0, h)),
                  pl.BlockSpec((nb, w), lambda h, i: (0, h))],
        out_specs=pl.BlockSpec((MOBA_BLOCK, w), lambda h, i: (i, h)),
        out_shape=jax.ShapeDtypeStruct((t, ATTN_WIDTH), BF16),
        scratch_shapes=[pltpu.VMEM((hp, t, 2 * HEAD_DIM), BF16),
                        pltpu.VMEM((hp, nb // ATTN_GROUP, HEAD_DIM, ATTN_GROUP * MOBA_BLOCK), BF16),
                        pltpu.VMEM((hp, nb, HEAD_DIM, MOBA_BLOCK), BF16),
                        pltpu.VMEM((hp, 2, ATTN_GROUP * MOBA_BLOCK, MOBA_BLOCK), F32),
                        pltpu.VMEM((hp, MOBA_BLOCK, MOBA_BLOCK), F32)],
        compiler_params=_params("arbitrary", "arbitrary"),
        name="attn_prompt",
    )(q, kb, v, km)


def _attn_sample_kernel(n_pages, pt_ref, q_ref, kn_ref, vn_ref, *refs):
    k_refs = refs[:n_pages]
    cv_ref, o_ref, s_ref, vbuf_ref, stat_ref, pick_ref, sem = refs[n_pages:]
    t = pl.program_id(0)
    n_seq = pl.num_programs(0) - 1
    page = k_refs[0].shape[0]
    pages_per_blk = MOBA_BLOCK // page
    n_blk = n_pages // pages_per_blk
    n_heads, sub = ATTN_HEADS, V7X_SUBLANES
    head_picks = [(h, r) for h in range(n_heads) for r in range(MOBA_TOPK)]

    ch = SAMPLE_CHUNK
    chunks = [(pg, c * ch) for pg in range(n_pages) for c in range(page // ch)]

    def slab_copy(slot, seq, h, r, blk_idx, part):
        page_id = pt_ref[seq, blk_idx * pages_per_blk + part]
        return pltpu.make_async_copy(cv_ref.at[0, page_id, :, h, :],
                                     vbuf_ref.at[slot, h, r, pl.ds(part * page, page), :],
                                     sem.at[slot])

    @pl.when(t < n_seq)
    def _():
        slot = jnp.bitwise_and(t, 1)
        q = q_ref[...]
        qe = q * EXP2_SCALE

        def score(k):
            return jnp.broadcast_to(jnp.sum(k * qe, axis=-1, keepdims=True), k.shape)

        ksum = [None] * n_blk
        smax = [None] * n_blk
        for pg, r0 in chunks:
            b = pg // pages_per_blk
            k = k_refs[pg][r0:r0 + ch]
            s = score(k)
            row0 = (pg * page + r0) * sub
            s_ref[slot, row0:row0 + ch * sub, :] = s.reshape(ch * sub, HEAD_DIM)
            ks, cm = jnp.sum(k, axis=0), jnp.max(s, axis=0)
            ksum[b] = ks if ksum[b] is None else ksum[b] + ks
            smax[b] = cm if smax[b] is None else jnp.maximum(smax[b], cm)

        kmean = jnp.concatenate([ks[None] for ks in ksum], axis=0) * (1.0 / MOBA_BLOCK)
        gate = jnp.sum(kmean * q[None], axis=-1, keepdims=True)
        bias, chosen = _top_blocks(gate, n_blk, n_blk)
        for n, (h, r) in enumerate(head_picks):
            blk_idx = jnp.minimum(chosen[r][0, h, 0], n_blk - 1)
            pick_ref[slot, n] = blk_idx
            for part in range(pages_per_blk):
                slab_copy(slot, t, h, r, blk_idx, part).start()

        s_new = score(kn_ref[...])
        m = s_new
        for b in range(n_blk):
            m = jnp.maximum(m, smax[b] + bias[b])
        stat_ref[slot, 0] = m
        stat_ref[slot, 1] = s_new

    @pl.when(t > 0)
    def _():
        slot = jnp.bitwise_and(t + 1, 1)
        picks = [(h, r, pick_ref[slot, n]) for n, (h, r) in enumerate(head_picks)]
        for h, r, blk_idx in picks:
            for part in range(pages_per_blk):
                slab_copy(slot, t - 1, h, r, blk_idx, part).wait()

        m, s_new = stat_ref[slot, 0], stat_ref[slot, 1]
        fold = lambda x: jnp.sum(x.reshape(MOBA_BLOCK // sub, sub, HEAD_DIM), axis=0)
        for h in range(n_heads):
            m_h = m[h:h + 1, :]
            p_new = jnp.exp2(s_new[h:h + 1, :] - m_h)
            l_h = jnp.zeros((sub, HEAD_DIM), F32)
            acc_h = jnp.zeros((sub, HEAD_DIM), F32)
            for hh, r, blk_idx in picks:
                if hh != h:
                    continue
                rows = pl.ds(blk_idx * (MOBA_BLOCK * sub) + h, MOBA_BLOCK, stride=sub)
                p = jnp.exp2(s_ref[slot, rows, :] - m_h)
                l_h = l_h + fold(p)
                acc_h = acc_h + fold(p * vbuf_ref[slot, h, r])
            l_row = jnp.sum(l_h, axis=0, keepdims=True) + p_new
            acc_row = jnp.sum(acc_h, axis=0, keepdims=True) + p_new * vn_ref[h:h + 1, :]
            o_ref[h:h + 1, :] = acc_row / l_row


def _attn_sample(q, k_new, v_new, cache_k, cache_v, page_table):
    n_seq, n_pages = page_table.shape
    page = cache_k.shape[2]
    assert MOBA_BLOCK % page == 0 and (n_pages * page) % MOBA_BLOCK == 0
    assert n_pages * page // MOBA_BLOCK >= MOBA_TOPK and ATTN_HEADS == V7X_SUBLANES
    lanes = (ATTN_HEADS, HEAD_DIM)
    last = n_seq - 1
    key_row = pl.BlockSpec((None,) + lanes, lambda t, pt: (jnp.minimum(t, last), 0, 0))
    val_row = pl.BlockSpec((None,) + lanes, lambda t, pt: (jnp.maximum(t - 1, 0), 0, 0))
    page_spec = lambda pg: pl.BlockSpec((None, None, page) + lanes,
                                        lambda t, pt: (0, pt[jnp.minimum(t, last), pg], 0, 0, 0))
    return pl.pallas_call(
        functools.partial(_attn_sample_kernel, n_pages),
        grid_spec=pltpu.PrefetchScalarGridSpec(
            num_scalar_prefetch=1,
            grid=(n_seq + 1,),
            in_specs=([key_row, key_row, val_row] + [page_spec(pg) for pg in range(n_pages)]
                      + [pl.BlockSpec(memory_space=pl.ANY)]),
            out_specs=val_row,
            scratch_shapes=[pltpu.VMEM((2, n_pages * page * ATTN_HEADS, HEAD_DIM), F32),
                            pltpu.VMEM((2, ATTN_HEADS, MOBA_TOPK, MOBA_BLOCK, HEAD_DIM), F32),
                            pltpu.VMEM((2, 2) + lanes, F32),
                            pltpu.SMEM((2, ATTN_HEADS * MOBA_TOPK), jnp.int32),
                            pltpu.SemaphoreType.DMA((2,))],
        ),
        out_shape=jax.ShapeDtypeStruct((n_seq,) + lanes, F32),
        compiler_params=_params("arbitrary"),
        name="attn_sample",
    )(page_table, q, k_new, v_new, *([cache_k] * n_pages), cache_v)


def _out_proj_kernel(x_ref, a_ref, c_ref, w_ref, g_ref, y_ref):
    mix = _dot(a_ref[...], w_ref[0:ATTN_WIDTH, :]) + _dot(c_ref[...], w_ref[ATTN_WIDTH:, :])
    y_ref[...] = x_ref[...] + _rms(mix, g_ref[...])


def _out_proj(x, attn, conv, w_out, g_post, *, tm):
    m, d = x.shape
    row = lambda width: pl.BlockSpec((tm, width), lambda i: (i, 0))
    return pl.pallas_call(
        _out_proj_kernel,
        grid=(m // tm,),
        in_specs=[row(d), row(ATTN_WIDTH), row(CONV_WIDTH),
                  pl.BlockSpec((d, d), lambda i: (0, 0)), pl.BlockSpec((1, d), lambda i: (0, 0))],
        out_specs=row(d),
        out_shape=jax.ShapeDtypeStruct((m, d), F32),
        compiler_params=_params("arbitrary"),
        name="out_proj",
    )(x, attn, conv, w_out, g_post)


def _ple_kernel(x_ref, hn_ref, p_ref, wg_ref, wp_ref, y_ref):
    gate = _dot(hn_ref[...], wg_ref[...])
    emb = _dot(p_ref[...].astype(BF16), wp_ref[...])
    y_ref[...] = x_ref[...] + (1.0 / (1.0 + jnp.exp(-gate))) * emb


def _ple(x, hn, p, w_gate, w_proj, *, tm, tn):
    m, d = x.shape
    pd = p.shape[1]
    return pl.pallas_call(
        _ple_kernel,
        grid=(m // tm, d // tn),
        in_specs=[pl.BlockSpec((tm, tn), lambda i, j: (i, j)),
                  pl.BlockSpec((tm, d), lambda i, j: (i, 0)),
                  pl.BlockSpec((tm, pd), lambda i, j: (i, 0)),
                  pl.BlockSpec((d, tn), lambda i, j: (0, j)),
                  pl.BlockSpec((pd, tn), lambda i, j: (0, j))],
        out_specs=pl.BlockSpec((tm, tn), lambda i, j: (i, j)),
        out_shape=jax.ShapeDtypeStruct((m, d), F32),
        compiler_params=_params("arbitrary", "arbitrary"),
        name="ple",
    )(x, hn, p, w_gate, w_proj)


def _tiles(m):
    tm = min(m, 512)
    return dict(tm=tm, tf=512, tn=ATTN_WIDTH, tp=D_MODEL)


def kernel(x_prompt, x_sample, cache_k, cache_v, state_conv, page_table, p_prompt, p_sample, w_in, w_out, w_conv, w_ffn1_gate, w_ffn1_up, w_ffn1_down, w_ffn2_gate, w_ffn2_up, w_ffn2_down, w_ple_gate, w_ple_proj, g_ffn1_pre, g_ffn1_post, g_mix_pre, g_mix_post, g_ffn2_pre, g_ffn2_post, g_ple):
    depth = w_in.shape[0]
    assert depth == 1 and x_prompt.shape[0] == 1 and x_sample.shape[1] == 1
    n_seq = x_sample.shape[0]
    seq = x_prompt.shape[1]
    n_phys, page = cache_k.shape[1], cache_k.shape[2]

    bf = lambda w: w[0].astype(BF16)
    w_in_b, w_out_b = bf(w_in), bf(w_out)
    w1g, w1u, w1d = bf(w_ffn1_gate), bf(w_ffn1_up), bf(w_ffn1_down)
    w2g, w2u, w2d = bf(w_ffn2_gate), bf(w_ffn2_up), bf(w_ffn2_down)
    w_pg, w_pp = bf(w_ple_gate), bf(w_ple_proj)
    wc = w_conv[0]

    tp, ts = _tiles(seq), _tiles(n_seq)
    heads = (ATTN_HEADS, HEAD_DIM)

    xp1, hp, xs1, hs = _ffn(x_prompt[0], x_sample[:, 0, :], g_ffn1_pre, g_ffn1_post, g_mix_pre,
                            w1g, w1u, w1d, tm=tp["tm"], tf=tp["tf"])

    qp, kp, vp, kbp, kmp = _qkv_proj(hp, w_in_b, tm=tp["tm"], with_block_means=True)
    conv_p, tail_p = _conv_prompt(hp, w_in_b, wc, tm=512, tc=512)
    attn_p = _attn_prompt(qp, kbp, vp, kmp.reshape(seq // MOBA_BLOCK, ATTN_WIDTH))
    xp2 = _out_proj(xp1, attn_p, conv_p, w_out_b, g_mix_post, tm=tp["tm"])

    qs, ks, vs = _qkv_proj(hs, w_in_b, tm=ts["tm"], with_block_means=False)
    state_t = jnp.transpose(state_conv[0], (1, 0, 2))
    conv_s, nstate_t = _conv_sample(hs, w_in_b, wc, state_t, tc=512)
    qs3, ks3, vs3 = (a.reshape(n_seq, *heads) for a in (qs, ks, vs))
    attn_s = _attn_sample(qs3, ks3, vs3, cache_k, cache_v, page_table)
    attn_s = attn_s.reshape(n_seq, ATTN_WIDTH).astype(BF16)
    xs2 = _out_proj(xs1, attn_s, conv_s, w_out_b, g_mix_post, tm=ts["tm"])

    xp3, hnp, xs3, hns = _ffn(xp2, xs2, g_ffn2_pre, g_ffn2_post, g_ple, w2g, w2u, w2d,
                              tm=tp["tm"], tf=tp["tf"])
    y_prompt = _ple(xp3, hnp, p_prompt[0, 0], w_pg, w_pp, tm=tp["tm"], tn=tp["tp"])
    y_sample = _ple(xs3, hns, p_sample[0, :, 0, :], w_pg, w_pp, tm=ts["tm"], tn=ts["tp"])

    return (y_prompt[None],
            y_sample[:, None, :],
            kp.reshape(1, 1, seq, *heads),
            vp.reshape(1, 1, seq, *heads),
            tail_p[V7X_SUBLANES - (CONV_K - 1):][None, None],
            ks.reshape(1, n_seq, 1, *heads),
            vs.reshape(1, n_seq, 1, *heads),
            jnp.transpose(nstate_t, (1, 0, 2))[None])
```

```python
import functools
import math

import jax
import jax.numpy as jnp
from jax import lax
from jax.experimental import pallas as pl
from jax.experimental.pallas import tpu as pltpu

F32 = jnp.float32
BF16 = jnp.bfloat16

D_MODEL = 2048
HEAD_DIM = 128
ATTN_HEADS = 8
ATTN_WIDTH = ATTN_HEADS * HEAD_DIM
CONV_WIDTH = D_MODEL - ATTN_WIDTH
CONV_K = 3
MOBA_BLOCK = 256
MOBA_TOPK = 3
RMS_EPS = 1e-6

V7X_SUBLANES = 8
V7X_LANES = 128
V7X_VMEM_LIMIT_BYTES = 56 * 1024 * 1024

ATTN_GROUP_LOG2 = 2
ATTN_GROUP = 1 << ATTN_GROUP_LOG2
ATTN_HEADS_PER_STEP = 2
SAMPLE_CHUNK = 8

NEG_BIG = -1e30
EXP2_SCALE = (HEAD_DIM ** -0.5) * math.log2(math.e)


def _params(*semantics):
    return pltpu.CompilerParams(dimension_semantics=semantics,
                                vmem_limit_bytes=V7X_VMEM_LIMIT_BYTES)


def _rms(x, g):
    var = jnp.mean(x * x, axis=-1, keepdims=True)
    return x * lax.rsqrt(var + RMS_EPS) * g


def _dot(a, b):
    return jnp.dot(a, b, preferred_element_type=F32)


def _dot_nt(a, b):
    return lax.dot_general(a, b, (((1,), (1,)), ((), ())), preferred_element_type=F32)


def _split3(x):
    hi = x.astype(BF16)
    r1 = x - hi.astype(F32)
    mid = r1.astype(BF16)
    lo = (r1 - mid.astype(F32)).astype(BF16)
    return hi, mid, lo


def _ffn_kernel(n_cast, x_ref, xs_ref, gpre_ref, gpost_ref, gnext_ref, wg_ref, wu_ref, wd_ref,
                *refs):
    n = n_cast
    w32_refs, (y_ref, hn_ref, ys_ref, hns_ref) = refs[:n], refs[n:n + 4]
    w16_refs = refs[n + 4:2 * n + 4]
    xn_ref, acc_ref, xns_ref, accs_ref = refs[2 * n + 4:]
    f = pl.program_id(1)

    def rows(x_ref, y_ref, hn_ref, xn_ref, acc_ref, side_casts=False):
        @pl.when(f == 0)
        def _():
            xn_ref[...] = _rms(x_ref[...], gpre_ref[...]).astype(BF16)
            acc_ref[...] = jnp.zeros_like(acc_ref)

        if side_casts:
            for w32_ref, w16_ref in zip(w32_refs, w16_refs):
                w16_ref[...] = w32_ref[...].astype(BF16)
        xn = xn_ref[...]
        g = _dot(xn, wg_ref[...])
        u = _dot(xn, wu_ref[...])
        a = (g * (1.0 / (1.0 + jnp.exp(-g))) * u).astype(BF16)
        acc_ref[...] += _dot(a, wd_ref[...])

        @pl.when(f == pl.num_programs(1) - 1)
        def _():
            y = x_ref[...] + 0.5 * _rms(acc_ref[...], gpost_ref[...])
            y_ref[...] = y
            hn_ref[...] = _rms(y, gnext_ref[...]).astype(BF16)

    rows(x_ref, y_ref, hn_ref, xn_ref, acc_ref, side_casts=True)

    @pl.when(pl.program_id(0) == pl.num_programs(0) - 1)
    def _():
        rows(xs_ref, ys_ref, hns_ref, xns_ref, accs_ref)


def _cast_tiling(w, n_row_blocks, n_steps):
    rows, cols = w.shape
    steps = max(s for s in range(1, n_steps + 1)
                if cols % s == 0 and (cols // s) % V7X_LANES == 0)
    return pl.BlockSpec((rows // n_row_blocks, cols // steps),
                        lambda i, f: (i, jnp.minimum(f, steps - 1)))


def _ffn(x, xs, g_pre, g_post, g_next, wg, wu, wd, *, tm, tf, cast=()):
    m, d = x.shape
    ms = xs.shape[0]
    d_ff = wg.shape[1]
    grid = (m // tm, d_ff // tf)
    row = pl.BlockSpec((tm, d), lambda i, f: (i, 0))
    small = pl.BlockSpec((ms, d), lambda i, f: (0, 0))
    vec = pl.BlockSpec((1, d), lambda i, f: (0, 0))
    cast_specs = [_cast_tiling(w, *grid) for w in cast]
    return pl.pallas_call(
        functools.partial(_ffn_kernel, len(cast)),
        grid=grid,
        in_specs=[row, small, vec, vec, vec,
                  pl.BlockSpec((d, tf), lambda i, f: (0, f)),
                  pl.BlockSpec((d, tf), lambda i, f: (0, f)),
                  pl.BlockSpec((tf, d), lambda i, f: (f, 0)), *cast_specs],
        out_specs=[row, row, small, small, *cast_specs],
        out_shape=[jax.ShapeDtypeStruct((m, d), F32), jax.ShapeDtypeStruct((m, d), BF16),
                   jax.ShapeDtypeStruct((ms, d), F32), jax.ShapeDtypeStruct((ms, d), BF16),
                   *[jax.ShapeDtypeStruct(w.shape, BF16) for w in cast]],
        scratch_shapes=[pltpu.VMEM((tm, d), BF16), pltpu.VMEM((tm, d), F32),
                        pltpu.VMEM((ms, d), BF16), pltpu.VMEM((ms, d), F32)],
        compiler_params=_params("arbitrary", "arbitrary"),
        name="ffn",
    )(x, xs, g_pre, g_post, g_next, wg, wu, wd, *cast)


def _qkv_kernel(h_ref, wq_ref, wk_ref, wv_ref, q_ref, k_ref, v_ref, *extra):
    h = h_ref[...]
    q_ref[...] = _dot(h, wq_ref[...])
    k = _dot(h, wk_ref[...])
    k_ref[...] = k
    v_ref[...] = _dot(h, wv_ref[...])
    if extra:
        kb_ref, km_ref = extra
        kb_ref[...] = k.astype(BF16)
        blocks, width = km_ref.shape
        km_ref[...] = jnp.sum(k.reshape(blocks, MOBA_BLOCK, width), axis=1) * (1.0 / MOBA_BLOCK)


def _qkv_proj(hb, w_in, *, tm, with_block_means):
    m, d = hb.shape
    tn = ATTN_WIDTH
    wspec = lambda part: pl.BlockSpec((d, tn), lambda i: (0, part))
    ospec = pl.BlockSpec((tm, tn), lambda i: (i, 0))
    f32o = jax.ShapeDtypeStruct((m, tn), F32)
    out_specs, out_shape = [ospec] * 3, [f32o] * 3
    if with_block_means:
        bpt = tm // MOBA_BLOCK
        out_specs = out_specs + [ospec, pl.BlockSpec((None, bpt, tn), lambda i: (i, 0, 0))]
        out_shape = out_shape + [jax.ShapeDtypeStruct((m, tn), BF16),
                                 jax.ShapeDtypeStruct((m // tm, bpt, tn), F32)]
    return pl.pallas_call(
        _qkv_kernel,
        grid=(m // tm,),
        in_specs=[pl.BlockSpec((tm, d), lambda i: (i, 0)), wspec(0), wspec(1), wspec(2)],
        out_specs=out_specs,
        out_shape=out_shape,
        compiler_params=_params("arbitrary"),
        name="qkv_proj",
    )(hb, w_in, w_in, w_in)


def _conv_prompt_kernel(h_ref, wb_ref, wc_ref, wu_ref, wconv_ref, conv_ref, tail_ref, zbuf_ref):
    i = pl.program_id(1)
    tm = h_ref.shape[0]
    halo = V7X_SUBLANES

    @pl.when(i == 0)
    def _():
        zbuf_ref[0:halo, :] = jnp.zeros((halo, zbuf_ref.shape[1]), F32)

    h = h_ref[...]
    z = _dot(h, wc_ref[...]) * _dot(h, wu_ref[...])
    zbuf_ref[halo:halo + tm, :] = z
    w = wconv_ref[...]
    y = zbuf_ref[halo:halo + tm, :] * w[CONV_K - 1:CONV_K, :]
    for j in range(CONV_K - 1):
        off = halo - (CONV_K - 1) + j
        y = y + zbuf_ref[off:off + tm, :] * w[j:j + 1, :]
    conv_ref[...] = (_dot(h, wb_ref[...]) * y).astype(BF16)
    last = zbuf_ref[tm:tm + halo, :]
    zbuf_ref[0:halo, :] = last
    tail_ref[...] = last


def _conv_prompt(hb, w_in, w_conv, *, tm, tc):
    m, d = hb.shape
    nj = CONV_WIDTH // tc
    base = 3 * ATTN_WIDTH // tc
    wspec = lambda part: pl.BlockSpec((d, tc), lambda j, i: (0, base + part * nj + j))
    return pl.pallas_call(
        _conv_prompt_kernel,
        grid=(nj, m // tm),
        in_specs=[pl.BlockSpec((tm, d), lambda j, i: (i, 0)), wspec(0), wspec(1), wspec(2),
                  pl.BlockSpec((CONV_K, tc), lambda j, i: (0, j))],
        out_specs=[pl.BlockSpec((tm, tc), lambda j, i: (i, j)),
                   pl.BlockSpec((V7X_SUBLANES, tc), lambda j, i: (0, j))],
        out_shape=[jax.ShapeDtypeStruct((m, CONV_WIDTH), BF16),
                   jax.ShapeDtypeStruct((V7X_SUBLANES, CONV_WIDTH), F32)],
        scratch_shapes=[pltpu.VMEM((tm + V7X_SUBLANES, tc), F32)],
        compiler_params=_params("arbitrary", "arbitrary"),
        name="conv_prompt",
    )(hb, w_in, w_in, w_in, w_conv)


def _conv_sample_kernel(h_ref, wb_ref, wc_ref, wu_ref, wconv_ref, st_ref, conv_ref, nst_ref):
    h = h_ref[...]
    z = _dot(h, wc_ref[...]) * _dot(h, wu_ref[...])
    w = wconv_ref[...]
    y = z * w[CONV_K - 1:CONV_K, :]
    for j in range(CONV_K - 1):
        y = y + st_ref[j] * w[j:j + 1, :]
    conv_ref[...] = (_dot(h, wb_ref[...]) * y).astype(BF16)
    for j in range(CONV_K - 2):
        nst_ref[j] = st_ref[j + 1]
    nst_ref[CONV_K - 2] = z


def _conv_sample(hb, w_in, w_conv, state_t, *, tc):
    m, d = hb.shape
    nj = CONV_WIDTH // tc
    base = 3 * ATTN_WIDTH // tc
    wspec = lambda part: pl.BlockSpec((d, tc), lambda j: (0, base + part * nj + j))
    sspec = pl.BlockSpec((CONV_K - 1, m, tc), lambda j: (0, 0, j))
    return pl.pallas_call(
        _conv_sample_kernel,
        grid=(nj,),
        in_specs=[pl.BlockSpec((m, d), lambda j: (0, 0)), wspec(0), wspec(1), wspec(2),
                  pl.BlockSpec((CONV_K, tc), lambda j: (0, j)), sspec],
        out_specs=[pl.BlockSpec((m, tc), lambda j: (0, j)), sspec],
        out_shape=[jax.ShapeDtypeStruct((m, CONV_WIDTH), BF16),
                   jax.ShapeDtypeStruct((CONV_K - 1, m, CONV_WIDTH), F32)],
        compiler_params=_params("arbitrary"),
        name="conv_sample",
    )(hb, w_in, w_in, w_in, w_conv, state_t)


def _top_blocks(gate, n_past, axis_len):
    blk = lax.broadcasted_iota(jnp.int32, gate.shape, 0)
    past = blk < n_past
    g = jnp.where(past, gate, -jnp.inf)
    keep = jnp.zeros(gate.shape, F32)
    chosen = []
    for _ in range(MOBA_TOPK):
        mx = jnp.max(g, axis=0, keepdims=True)
        idx = jnp.min(jnp.where(g == mx, blk, axis_len), axis=0, keepdims=True)
        pick = blk == idx
        keep = jnp.where(pick, jnp.where(past, 1.0, keep), keep)
        g = jnp.where(pick, -jnp.inf, g)
        chosen.append(idx)
    return jnp.where(keep > 0.0, 0.0, NEG_BIG), chosen


def _top_blocks_bias(gate, n_past, axis_len):
    return _top_blocks(gate, n_past, axis_len)[0]


def _attn_prompt_kernel(q_ref, kb_ref, v_ref, km_ref, o_ref, kx_ref, vt_ref, vtb_ref, s_ref, sd_ref):
    i = pl.program_id(1)
    n_heads, n_grp = vt_ref.shape[0], vt_ref.shape[1]
    blk = MOBA_BLOCK
    grp = ATTN_GROUP
    gk = grp * blk
    nb = n_grp * grp
    dh = HEAD_DIM

    @pl.when(i == 0)
    def _():
        col = lax.broadcasted_iota(jnp.int32, (blk, dh), 1)
        for h in range(n_heads):
            for c in range(nb):
                rows = slice(c * blk, (c + 1) * blk)
                lane0 = (c % grp) * blk
                vt = v_ref[rows, h * dh:(h + 1) * dh].T.astype(BF16)
                vt_ref[h, c // grp, :, lane0:lane0 + blk] = vt
                vtb_ref[h, c] = vt
                kx_ref[h, rows, 0:dh] = kb_ref[rows, h * dh:(h + 1) * dh]
                kx_ref[h, rows, dh:2 * dh] = jnp.where(col == c, 1.0, 0.0).astype(BF16)

    def scores(h, g, q_ext):
        kg = kx_ref[h, pl.ds(pl.multiple_of(g * gk, gk), gk), :]
        return _dot_nt(kg, q_ext)

    def update(h, g, slot, state):
        m, l, acc = state
        s = s_ref[h, slot]
        m_new = jnp.maximum(m, jnp.max(s, axis=0, keepdims=True))
        p = jnp.exp2(s - m_new)
        alpha = jnp.exp2(m - m_new)
        l = alpha * l + jnp.sum(p, axis=0, keepdims=True)
        acc = acc * alpha + _dot(vt_ref[h, g], p.astype(BF16))
        return m_new, l, acc

    qbs, gates, s0s, q_exts, state = [], [], [], [], []
    for h in range(n_heads):
        q = q_ref[:, h * dh:(h + 1) * dh]
        qbs.append((q * EXP2_SCALE).astype(BF16))
        q_hi, q_mid, _ = _split3(q)
        k_hi, k_mid, _ = _split3(km_ref[:, h * dh:(h + 1) * dh])
        gates.append(_dot_nt(k_hi, q_hi) + (_dot_nt(k_hi, q_mid) + _dot_nt(k_mid, q_hi)))

    for h in range(n_heads):
        s0s.append(_dot_nt(kb_ref[0:gk, h * dh:(h + 1) * dh], qbs[h]))
        k_own = kb_ref[pl.ds(pl.multiple_of(i * blk, blk), blk), h * dh:(h + 1) * dh]
        s = _dot_nt(k_own, qbs[h])
        key_pos = lax.broadcasted_iota(jnp.int32, s.shape, 0)
        qry_pos = lax.broadcasted_iota(jnp.int32, s.shape, 1)
        sd_ref[h] = jnp.where(key_pos <= qry_pos, s, NEG_BIG)

    for h in range(n_heads):
        bias = _top_blocks_bias(gates[h], i, nb)
        s_ref[h, 0] = (s0s[h].reshape(grp, blk, blk) + bias[0:grp][:, None, :]).reshape(gk, blk)
        bias = jnp.concatenate([bias, jnp.zeros((dh - nb, blk), F32)], axis=0)
        q_exts.append(jnp.concatenate([qbs[h], bias.T.astype(BF16)], axis=1))
        state.append((jnp.full((1, blk), NEG_BIG, F32), jnp.zeros((1, blk), F32),
                      jnp.zeros((dh, blk), F32)))

    n_groups = lax.shift_right_logical(i + (grp - 1), ATTN_GROUP_LOG2)
    n_pairs = lax.shift_right_logical(n_groups, 1)

    def body(u, carry):
        g0 = 2 * u
        for h in range(n_heads):
            s_ref[h, 1] = scores(h, g0 + 1, q_exts[h])
        carry = tuple(update(h, g0, 0, carry[h]) for h in range(n_heads))
        for h in range(n_heads):
            s_ref[h, 0] = scores(h, jnp.minimum(g0 + 2, n_grp - 1), q_exts[h])
        return tuple(update(h, g0 + 1, 1, carry[h]) for h in range(n_heads))

    state = lax.fori_loop(0, n_pairs, body, tuple(state))

    def finish(with_group):
        for h in range(n_heads):
            m, l, acc = state[h]
            sd = sd_ref[h]
            m_new = jnp.maximum(m, jnp.max(sd, axis=0, keepdims=True))
            if with_group:
                s = s_ref[h, 0]
                m_new = jnp.maximum(m_new, jnp.max(s, axis=0, keepdims=True))
            pd = jnp.exp2(sd - m_new)
            alpha = jnp.exp2(m - m_new)
            l = alpha * l + jnp.sum(pd, axis=0, keepdims=True)
            acc = acc * alpha + _dot(vtb_ref[h, i], pd.astype(BF16))
            if with_group:
                p = jnp.exp2(s - m_new)
                l = l + jnp.sum(p, axis=0, keepdims=True)
                acc = acc + _dot(vt_ref[h, 2 * n_pairs], p.astype(BF16))
            o_ref[:, h * dh:(h + 1) * dh] = (acc / l).T.astype(BF16)

    odd = 2 * n_pairs < n_groups
    pl.when(odd)(lambda: finish(True))
    pl.when(jnp.logical_not(odd))(lambda: finish(False))


def _attn_prompt(q, kb, v, km):
    t = q.shape[0]
    nb = t // MOBA_BLOCK
    hp = ATTN_HEADS_PER_STEP
    assert nb % ATTN_GROUP == 0 and ATTN_HEADS % hp == 0
    w = hp * HEAD_DIM
    return pl.pallas_call(
        _attn_prompt_kernel,
        grid=(ATTN_HEADS // hp, nb),
        in_specs=[pl.BlockSpec((MOBA_BLOCK, w), lambda h, i: (i, h)),
                  pl.BlockSpec((t, w), lambda h, i: (0, h)),
                  pl.BlockSpec((t, w), lambda h, i: (0, h)),
                  pl.BlockSpec((nb, w), lambda h, i: (0, h))],
        out_specs=pl.BlockSpec((MOBA_BLOCK, w), lambda h, i: (i, h)),
        out_shape=jax.ShapeDtypeStruct((t, ATTN_WIDTH), BF16),
        scratch_shapes=[pltpu.VMEM((hp, t, 2 * HEAD_DIM), BF16),
                        pltpu.VMEM((hp, nb // ATTN_GROUP, HEAD_DIM, ATTN_GROUP * MOBA_BLOCK), BF16),
                        pltpu.VMEM((hp, nb, HEAD_DIM, MOBA_BLOCK), BF16),
                        pltpu.VMEM((hp, 2, ATTN_GROUP * MOBA_BLOCK, MOBA_BLOCK), F32),
                        pltpu.VMEM((hp, MOBA_BLOCK, MOBA_BLOCK), F32)],
        compiler_params=_params("arbitrary", "arbitrary"),
        name="attn_prompt",
    )(q, kb, v, km)


def _attn_sample_kernel(n_pages, pt_ref, q_ref, kn_ref, vn_ref, *refs):
    k_refs = refs[:n_pages]
    cv_ref, o_ref, s_ref, vbuf_ref, stat_ref, pick_ref, sem = refs[n_pages:]
    t = pl.program_id(0)
    n_seq = pl.num_programs(0) - 1
    page = k_refs[0].shape[0]
    pages_per_blk = MOBA_BLOCK // page
    n_blk = n_pages // pages_per_blk
    n_heads, sub = ATTN_HEADS, V7X_SUBLANES
    head_picks = [(h, r) for h in range(n_heads) for r in range(MOBA_TOPK)]

    ch = SAMPLE_CHUNK
    chunks = [(pg, c * ch) for pg in range(n_pages) for c in range(page // ch)]

    def slab_copy(slot, seq, h, r, blk_idx, part):
        page_id = pt_ref[seq, blk_idx * pages_per_blk + part]
        return pltpu.make_async_copy(cv_ref.at[0, page_id, :, h, :],
                                     vbuf_ref.at[slot, h, r, pl.ds(part * page, page), :],
                                     sem.at[slot])

    @pl.when(t < n_seq)
    def _():
        slot = jnp.bitwise_and(t, 1)
        q = q_ref[...]
        qe = q * EXP2_SCALE

        def score(k):
            return jnp.broadcast_to(jnp.sum(k * qe, axis=-1, keepdims=True), k.shape)

        ksum = [None] * n_blk
        smax = [None] * n_blk
        for pg, r0 in chunks:
            b = pg // pages_per_blk
            k = k_refs[pg][r0:r0 + ch]
            s = score(k)
            row0 = (pg * page + r0) * sub
            s_ref[slot, row0:row0 + ch * sub, :] = s.reshape(ch * sub, HEAD_DIM)
            ks, cm = jnp.sum(k, axis=0), jnp.max(s, axis=0)
            ksum[b] = ks if ksum[b] is None else ksum[b] + ks
            smax[b] = cm if smax[b] is None else jnp.maximum(smax[b], cm)

        kmean = jnp.concatenate([ks[None] for ks in ksum], axis=0) * (1.0 / MOBA_BLOCK)
        gate = jnp.sum(kmean * q[None], axis=-1, keepdims=True)
        bias, chosen = _top_blocks(gate, n_blk, n_blk)
        for n, (h, r) in enumerate(head_picks):
            blk_idx = jnp.minimum(chosen[r][0, h, 0], n_blk - 1)
            pick_ref[slot, n] = blk_idx
            for part in range(pages_per_blk):
                slab_copy(slot, t, h, r, blk_idx, part).start()

        s_new = score(kn_ref[...])
        m = s_new
        for b in range(n_blk):
            m = jnp.maximum(m, smax[b] + bias[b])
        stat_ref[slot, 0] = m
        stat_ref[slot, 1] = s_new

    @pl.when(t > 0)
    def _():
        slot = jnp.bitwise_and(t + 1, 1)
        picks = [(h, r, pick_ref[slot, n]) for n, (h, r) in enumerate(head_picks)]
        for h, r, blk_idx in picks:
            for part in range(pages_per_blk):
                slab_copy(slot, t - 1, h, r, blk_idx, part).wait()

        m, s_new = stat_ref[slot, 0], stat_ref[slot, 1]
        fold = lambda x: jnp.sum(x.reshape(MOBA_BLOCK // sub, sub, HEAD_DIM), axis=0)
        for h in range(n_heads):
            m_h = m[h:h + 1, :]
            p_new = jnp.exp2(s_new[h:h + 1, :] - m_h)
            l_h = jnp.zeros((sub, HEAD_DIM), F32)
            acc_h = jnp.zeros((sub, HEAD_DIM), F32)
            for hh, r, blk_idx in picks:
                if hh != h:
                    continue
                rows = pl.ds(blk_idx * (MOBA_BLOCK * sub) + h, MOBA_BLOCK, stride=sub)
                p = jnp.exp2(s_ref[slot, rows, :] - m_h)
                l_h = l_h + fold(p)
                acc_h = acc_h + fold(p * vbuf_ref[slot, h, r])
            l_row = jnp.sum(l_h, axis=0, keepdims=True) + p_new
            acc_row = jnp.sum(acc_h, axis=0, keepdims=True) + p_new * vn_ref[h:h + 1, :]
            o_ref[h:h + 1, :] = acc_row / l_row


def _attn_sample(q, k_new, v_new, cache_k, cache_v, page_table):
    n_seq, n_pages = page_table.shape
    page = cache_k.shape[2]
    assert MOBA_BLOCK % page == 0 and (n_pages * page) % MOBA_BLOCK == 0
    assert n_pages * page // MOBA_BLOCK >= MOBA_TOPK and ATTN_HEADS == V7X_SUBLANES
    lanes = (ATTN_HEADS, HEAD_DIM)
    last = n_seq - 1
    key_row = pl.BlockSpec((None,) + lanes, lambda t, pt: (jnp.minimum(t, last), 0, 0))
    val_row = pl.BlockSpec((None,) + lanes, lambda t, pt: (jnp.maximum(t - 1, 0), 0, 0))
    page_spec = lambda pg: pl.BlockSpec((None, None, page) + lanes,
                                        lambda t, pt: (0, pt[jnp.minimum(t, last), pg], 0, 0, 0))
    return pl.pallas_call(
        functools.partial(_attn_sample_kernel, n_pages),
        grid_spec=pltpu.PrefetchScalarGridSpec(
            num_scalar_prefetch=1,
            grid=(n_seq + 1,),
            in_specs=([key_row, key_row, val_row] + [page_spec(pg) for pg in range(n_pages)]
                      + [pl.BlockSpec(memory_space=pl.ANY)]),
            out_specs=val_row,
            scratch_shapes=[pltpu.VMEM((2, n_pages * page * ATTN_HEADS, HEAD_DIM), F32),
                            pltpu.VMEM((2, ATTN_HEADS, MOBA_TOPK, MOBA_BLOCK, HEAD_DIM), F32),
                            pltpu.VMEM((2, 2) + lanes, F32),
                            pltpu.SMEM((2, ATTN_HEADS * MOBA_TOPK), jnp.int32),
                            pltpu.SemaphoreType.DMA((2,))],
        ),
        out_shape=jax.ShapeDtypeStruct((n_seq,) + lanes, F32),
        compiler_params=_params("arbitrary"),
        name="attn_sample",
    )(page_table, q, k_new, v_new, *([cache_k] * n_pages), cache_v)


def _out_proj_kernel(x_ref, a_ref, c_ref, w_ref, g_ref, y_ref):
    mix = _dot(a_ref[...], w_ref[0:ATTN_WIDTH, :]) + _dot(c_ref[...], w_ref[ATTN_WIDTH:, :])
    y_ref[...] = x_ref[...] + _rms(mix, g_ref[...])


def _out_proj(x, attn, conv, w_out, g_post, *, tm):
    m, d = x.shape
    row = lambda width: pl.BlockSpec((tm, width), lambda i: (i, 0))
    return pl.pallas_call(
        _out_proj_kernel,
        grid=(m // tm,),
        in_specs=[row(d), row(ATTN_WIDTH), row(CONV_WIDTH),
                  pl.BlockSpec((d, d), lambda i: (0, 0)), pl.BlockSpec((1, d), lambda i: (0, 0))],
        out_specs=row(d),
        out_shape=jax.ShapeDtypeStruct((m, d), F32),
        compiler_params=_params("arbitrary"),
        name="out_proj",
    )(x, attn, conv, w_out, g_post)


def _ple_kernel(x_ref, hn_ref, p_ref, wg_ref, wp_ref, y_ref):
    gate = _dot(hn_ref[...], wg_ref[...])
    emb = _dot(p_ref[...].astype(BF16), wp_ref[...])
    y_ref[...] = x_ref[...] + (1.0 / (1.0 + jnp.exp(-gate))) * emb


def _ple(x, hn, p, w_gate, w_proj, *, tm, tn):
    m, d = x.shape
    pd = p.shape[1]
    return pl.pallas_call(
        _ple_kernel,
        grid=(m // tm, d // tn),
        in_specs=[pl.BlockSpec((tm, tn), lambda i, j: (i, j)),
                  pl.BlockSpec((tm, d), lambda i, j: (i, 0)),
                  pl.BlockSpec((tm, pd), lambda i, j: (i, 0)),
                  pl.BlockSpec((d, tn), lambda i, j: (0, j)),
                  pl.BlockSpec((pd, tn), lambda i, j: (0, j))],
        out_specs=pl.BlockSpec((tm, tn), lambda i, j: (i, j)),
        out_shape=jax.ShapeDtypeStruct((m, d), F32),
        compiler_params=_params("arbitrary", "arbitrary"),
        name="ple",
    )(x, hn, p, w_gate, w_proj)


def _tiles(m):
    tm = min(m, 512)
    return dict(tm=tm, tf=512, tn=ATTN_WIDTH, tp=D_MODEL)


def kernel(x_prompt, x_sample, cache_k, cache_v, state_conv, page_table, p_prompt, p_sample, w_in, w_out, w_conv, w_ffn1_gate, w_ffn1_up, w_ffn1_down, w_ffn2_gate, w_ffn2_up, w_ffn2_down, w_ple_gate, w_ple_proj, g_ffn1_pre, g_ffn1_post, g_mix_pre, g_mix_post, g_ffn2_pre, g_ffn2_post, g_ple):
    depth = w_in.shape[0]
    assert depth == 1 and x_prompt.shape[0] == 1 and x_sample.shape[1] == 1
    n_seq = x_sample.shape[0]
    seq = x_prompt.shape[1]
    n_phys, page = cache_k.shape[1], cache_k.shape[2]

    bf = lambda w: w[0].astype(BF16)
    w1g, w1u, w1d = bf(w_ffn1_gate), bf(w_ffn1_up), bf(w_ffn1_down)
    wc = w_conv[0]

    tp, ts = _tiles(seq), _tiles(n_seq)
    heads = (ATTN_HEADS, HEAD_DIM)

    later = (w_ffn2_gate[0], w_ffn2_up[0], w_ffn2_down[0], w_in[0], w_out[0],
             w_ple_gate[0], w_ple_proj[0])
    xp1, hp, xs1, hs, w2g, w2u, w2d, w_in_b, w_out_b, w_pg, w_pp = _ffn(
        x_prompt[0], x_sample[:, 0, :], g_ffn1_pre, g_ffn1_post, g_mix_pre,
        w1g, w1u, w1d, tm=tp["tm"], tf=tp["tf"], cast=later)

    qp, kp, vp, kbp, kmp = _qkv_proj(hp, w_in_b, tm=tp["tm"], with_block_means=True)
    conv_p, tail_p = _conv_prompt(hp, w_in_b, wc, tm=512, tc=512)
    attn_p = _attn_prompt(qp, kbp, vp, kmp.reshape(seq // MOBA_BLOCK, ATTN_WIDTH))
    xp2 = _out_proj(xp1, attn_p, conv_p, w_out_b, g_mix_post, tm=tp["tm"])

    qs, ks, vs = _qkv_proj(hs, w_in_b, tm=ts["tm"], with_block_means=False)
    state_t = jnp.transpose(state_conv[0], (1, 0, 2))
    conv_s, nstate_t = _conv_sample(hs, w_in_b, wc, state_t, tc=512)
    qs3, ks3, vs3 = (a.reshape(n_seq, *heads) for a in (qs, ks, vs))
    attn_s = _attn_sample(qs3, ks3, vs3, cache_k, cache_v, page_table)
    attn_s = attn_s.reshape(n_seq, ATTN_WIDTH).astype(BF16)
    xs2 = _out_proj(xs1, attn_s, conv_s, w_out_b, g_mix_post, tm=ts["tm"])

    xp3, hnp, xs3, hns = _ffn(xp2, xs2, g_ffn2_pre, g_ffn2_post, g_ple, w2g, w2u, w2d,
                              tm=tp["tm"], tf=tp["tf"])
    y_prompt = _ple(xp3, hnp, p_prompt[0, 0], w_pg, w_pp, tm=tp["tm"], tn=tp["tp"])
    y_sample = _ple(xs3, hns, p_sample[0, :, 0, :], w_pg, w_pp, tm=ts["tm"], tn=ts["tp"])

    return (y_prompt[None],
            y_sample[:, None, :],
            kp.reshape(1, 1, seq, *heads),
            vp.reshape(1, 1, seq, *heads),
            tail_p[V7X_SUBLANES - (CONV_K - 1):][None, None],
            ks.reshape(1, n_seq, 1, *heads),
            vs.reshape(1, n_seq, 1, *heads),
            jnp.transpose(nstate_t, (1, 0, 2))[None])
```

```python
import functools
import math

import jax
import jax.numpy as jnp
from jax import lax
from jax.experimental import pallas as pl
from jax.experimental.pallas import tpu as pltpu

F32 = jnp.float32
BF16 = jnp.bfloat16

D_MODEL = 2048
HEAD_DIM = 128
ATTN_HEADS = 8
ATTN_WIDTH = ATTN_HEADS * HEAD_DIM
CONV_WIDTH = D_MODEL - ATTN_WIDTH
CONV_K = 3
MOBA_BLOCK = 256
MOBA_TOPK = 3
RMS_EPS = 1e-6

V7X_SUBLANES = 8
V7X_LANES = 128
V7X_VMEM_LIMIT_BYTES = 56 * 1024 * 1024

FFN_EDGE_CHUNK = 256
ATTN_GROUP_LOG2 = 2
ATTN_GROUP = 1 << ATTN_GROUP_LOG2
ATTN_HEADS_PER_STEP = 2
SAMPLE_CHUNK = 8

NEG_BIG = -1e30
EXP2_SCALE = (HEAD_DIM ** -0.5) * math.log2(math.e)


def _params(*semantics):
    return pltpu.CompilerParams(dimension_semantics=semantics,
                                vmem_limit_bytes=V7X_VMEM_LIMIT_BYTES)


def _rms(x, g):
    var = jnp.mean(x * x, axis=-1, keepdims=True)
    return x * lax.rsqrt(var + RMS_EPS) * g


def _dot(a, b):
    return jnp.dot(a, b, preferred_element_type=F32)


def _dot_nt(a, b):
    return lax.dot_general(a, b, (((1,), (1,)), ((), ())), preferred_element_type=F32)


def _split3(x):
    hi = x.astype(BF16)
    r1 = x - hi.astype(F32)
    mid = r1.astype(BF16)
    lo = (r1 - mid.astype(F32)).astype(BF16)
    return hi, mid, lo


def _ffn_kernel(n_cast, x_ref, xs_ref, gpre_ref, gpost_ref, gnext_ref, wg_ref, wu_ref, wd_ref,
                *refs):
    n = n_cast
    w32_refs, (y_ref, hn_ref, ys_ref, hns_ref) = refs[:n], refs[n:n + 4]
    w16_refs = refs[n + 4:2 * n + 4]
    xn_ref, acc_ref, xns_ref, accs_ref = refs[2 * n + 4:]
    f = pl.program_id(1)

    def rows(x_ref, y_ref, hn_ref, xn_ref, acc_ref, side_casts=False):
        last = pl.num_programs(1) - 1
        n_rows = x_ref.shape[0]
        chunk = min(n_rows, FFN_EDGE_CHUNK)
        chunks = [slice(r, r + chunk) for r in range(0, n_rows, chunk)]

        def casts():
            if side_casts:
                for w32_ref, w16_ref in zip(w32_refs, w16_refs):
                    w16_ref[...] = w32_ref[...].astype(BF16)

        def mlp(xn):
            g = _dot(xn, wg_ref[...])
            u = _dot(xn, wu_ref[...])
            a = (g * (1.0 / (1.0 + jnp.exp(-g))) * u).astype(BF16)
            return _dot(a, wd_ref[...])

        @pl.when(f == 0)
        def _():
            casts()
            for c in chunks:
                xn = _rms(x_ref[c, :], gpre_ref[...]).astype(BF16)
                xn_ref[c, :] = xn
                acc_ref[c, :] = mlp(xn)

        @pl.when(jnp.logical_and(f > 0, f < last))
        def _():
            casts()
            acc_ref[...] += mlp(xn_ref[...])

        @pl.when(f == last)
        def _():
            casts()
            for c in chunks:
                acc = acc_ref[c, :] + mlp(xn_ref[c, :])
                y = x_ref[c, :] + 0.5 * _rms(acc, gpost_ref[...])
                y_ref[c, :] = y
                hn_ref[c, :] = _rms(y, gnext_ref[...]).astype(BF16)

    rows(x_ref, y_ref, hn_ref, xn_ref, acc_ref, side_casts=True)

    @pl.when(pl.program_id(0) == pl.num_programs(0) - 1)
    def _():
        rows(xs_ref, ys_ref, hns_ref, xns_ref, accs_ref)


def _cast_tiling(w, n_row_blocks, n_steps):
    rows, cols = w.shape
    steps = max(s for s in range(1, n_steps + 1)
                if cols % s == 0 and (cols // s) % V7X_LANES == 0)
    return pl.BlockSpec((rows // n_row_blocks, cols // steps),
                        lambda i, f: (i, jnp.minimum(f, steps - 1)))


def _ffn(x, xs, g_pre, g_post, g_next, wg, wu, wd, *, tm, tf, cast=()):
    m, d = x.shape
    ms = xs.shape[0]
    d_ff = wg.shape[1]
    grid = (m // tm, d_ff // tf)
    row = pl.BlockSpec((tm, d), lambda i, f: (i, 0))
    small = pl.BlockSpec((ms, d), lambda i, f: (0, 0))
    vec = pl.BlockSpec((1, d), lambda i, f: (0, 0))
    cast_specs = [_cast_tiling(w, *grid) for w in cast]
    return pl.pallas_call(
        functools.partial(_ffn_kernel, len(cast)),
        grid=grid,
        in_specs=[row, small, vec, vec, vec,
                  pl.BlockSpec((d, tf), lambda i, f: (0, f)),
                  pl.BlockSpec((d, tf), lambda i, f: (0, f)),
                  pl.BlockSpec((tf, d), lambda i, f: (f, 0)), *cast_specs],
        out_specs=[row, row, small, small, *cast_specs],
        out_shape=[jax.ShapeDtypeStruct((m, d), F32), jax.ShapeDtypeStruct((m, d), BF16),
                   jax.ShapeDtypeStruct((ms, d), F32), jax.ShapeDtypeStruct((ms, d), BF16),
                   *[jax.ShapeDtypeStruct(w.shape, BF16) for w in cast]],
        scratch_shapes=[pltpu.VMEM((tm, d), BF16), pltpu.VMEM((tm, d), F32),
                        pltpu.VMEM((ms, d), BF16), pltpu.VMEM((ms, d), F32)],
        compiler_params=_params("arbitrary", "arbitrary"),
        name="ffn",
    )(x, xs, g_pre, g_post, g_next, wg, wu, wd, *cast)


def _qkv_kernel(h_ref, wq_ref, wk_ref, wv_ref, q_ref, k_ref, v_ref, *extra):
    h = h_ref[...]
    q_ref[...] = _dot(h, wq_ref[...])
    k = _dot(h, wk_ref[...])
    k_ref[...] = k
    v_ref[...] = _dot(h, wv_ref[...])
    if extra:
        kb_ref, km_ref = extra
        kb_ref[...] = k.astype(BF16)
        blocks, width = km_ref.shape
        km_ref[...] = jnp.sum(k.reshape(blocks, MOBA_BLOCK, width), axis=1) * (1.0 / MOBA_BLOCK)


def _qkv_proj(hb, w_in, *, tm, with_block_means):
    m, d = hb.shape
    tn = ATTN_WIDTH
    wspec = lambda part: pl.BlockSpec((d, tn), lambda i: (0, part))
    ospec = pl.BlockSpec((tm, tn), lambda i: (i, 0))
    f32o = jax.ShapeDtypeStruct((m, tn), F32)
    out_specs, out_shape = [ospec] * 3, [f32o] * 3
    if with_block_means:
        bpt = tm // MOBA_BLOCK
        out_specs = out_specs + [ospec, pl.BlockSpec((None, bpt, tn), lambda i: (i, 0, 0))]
        out_shape = out_shape + [jax.ShapeDtypeStruct((m, tn), BF16),
                                 jax.ShapeDtypeStruct((m // tm, bpt, tn), F32)]
    return pl.pallas_call(
        _qkv_kernel,
        grid=(m // tm,),
        in_specs=[pl.BlockSpec((tm, d), lambda i: (i, 0)), wspec(0), wspec(1), wspec(2)],
        out_specs=out_specs,
        out_shape=out_shape,
        compiler_params=_params("arbitrary"),
        name="qkv_proj",
    )(hb, w_in, w_in, w_in)


def _conv_prompt_kernel(h_ref, wb_ref, wc_ref, wu_ref, wconv_ref, conv_ref, tail_ref, zbuf_ref):
    i = pl.program_id(1)
    tm = h_ref.shape[0]
    halo = V7X_SUBLANES

    @pl.when(i == 0)
    def _():
        zbuf_ref[0:halo, :] = jnp.zeros((halo, zbuf_ref.shape[1]), F32)

    h = h_ref[...]
    z = _dot(h, wc_ref[...]) * _dot(h, wu_ref[...])
    zbuf_ref[halo:halo + tm, :] = z
    w = wconv_ref[...]
    y = zbuf_ref[halo:halo + tm, :] * w[CONV_K - 1:CONV_K, :]
    for j in range(CONV_K - 1):
        off = halo - (CONV_K - 1) + j
        y = y + zbuf_ref[off:off + tm, :] * w[j:j + 1, :]
    conv_ref[...] = (_dot(h, wb_ref[...]) * y).astype(BF16)
    last = zbuf_ref[tm:tm + halo, :]
    zbuf_ref[0:halo, :] = last
    tail_ref[...] = last


def _conv_prompt(hb, w_in, w_conv, *, tm, tc):
    m, d = hb.shape
    nj = CONV_WIDTH // tc
    base = 3 * ATTN_WIDTH // tc
    wspec = lambda part: pl.BlockSpec((d, tc), lambda j, i: (0, base + part * nj + j))
    return pl.pallas_call(
        _conv_prompt_kernel,
        grid=(nj, m // tm),
        in_specs=[pl.BlockSpec((tm, d), lambda j, i: (i, 0)), wspec(0), wspec(1), wspec(2),
                  pl.BlockSpec((CONV_K, tc), lambda j, i: (0, j))],
        out_specs=[pl.BlockSpec((tm, tc), lambda j, i: (i, j)),
                   pl.BlockSpec((V7X_SUBLANES, tc), lambda j, i: (0, j))],
        out_shape=[jax.ShapeDtypeStruct((m, CONV_WIDTH), BF16),
                   jax.ShapeDtypeStruct((V7X_SUBLANES, CONV_WIDTH), F32)],
        scratch_shapes=[pltpu.VMEM((tm + V7X_SUBLANES, tc), F32)],
        compiler_params=_params("arbitrary", "arbitrary"),
        name="conv_prompt",
    )(hb, w_in, w_in, w_in, w_conv)


def _conv_sample_kernel(h_ref, wb_ref, wc_ref, wu_ref, wconv_ref, st_ref, conv_ref, nst_ref):
    h = h_ref[...]
    z = _dot(h, wc_ref[...]) * _dot(h, wu_ref[...])
    w = wconv_ref[...]
    y = z * w[CONV_K - 1:CONV_K, :]
    for j in range(CONV_K - 1):
        y = y + st_ref[j] * w[j:j + 1, :]
    conv_ref[...] = (_dot(h, wb_ref[...]) * y).astype(BF16)
    for j in range(CONV_K - 2):
        nst_ref[j] = st_ref[j + 1]
    nst_ref[CONV_K - 2] = z


def _conv_sample(hb, w_in, w_conv, state_t, *, tc):
    m, d = hb.shape
    nj = CONV_WIDTH // tc
    base = 3 * ATTN_WIDTH // tc
    wspec = lambda part: pl.BlockSpec((d, tc), lambda j: (0, base + part * nj + j))
    sspec = pl.BlockSpec((CONV_K - 1, m, tc), lambda j: (0, 0, j))
    return pl.pallas_call(
        _conv_sample_kernel,
        grid=(nj,),
        in_specs=[pl.BlockSpec((m, d), lambda j: (0, 0)), wspec(0), wspec(1), wspec(2),
                  pl.BlockSpec((CONV_K, tc), lambda j: (0, j)), sspec],
        out_specs=[pl.BlockSpec((m, tc), lambda j: (0, j)), sspec],
        out_shape=[jax.ShapeDtypeStruct((m, CONV_WIDTH), BF16),
                   jax.ShapeDtypeStruct((CONV_K - 1, m, CONV_WIDTH), F32)],
        compiler_params=_params("arbitrary"),
        name="conv_sample",
    )(hb, w_in, w_in, w_in, w_conv, state_t)


def _top_blocks(gate, n_past, axis_len):
    blk = lax.broadcasted_iota(jnp.int32, gate.shape, 0)
    past = blk < n_past
    g = jnp.where(past, gate, -jnp.inf)
    keep = jnp.zeros(gate.shape, F32)
    chosen = []
    for _ in range(MOBA_TOPK):
        mx = jnp.max(g, axis=0, keepdims=True)
        idx = jnp.min(jnp.where(g == mx, blk, axis_len), axis=0, keepdims=True)
        pick = blk == idx
        keep = jnp.where(pick, jnp.where(past, 1.0, keep), keep)
        g = jnp.where(pick, -jnp.inf, g)
        chosen.append(idx)
    return jnp.where(keep > 0.0, 0.0, NEG_BIG), chosen


def _top_blocks_bias(gate, n_past, axis_len):
    return _top_blocks(gate, n_past, axis_len)[0]


def _attn_prompt_kernel(q_ref, kb_ref, v_ref, km_ref, o_ref, kx_ref, vt_ref, vtb_ref, s_ref, sd_ref):
    i = pl.program_id(1)
    n_heads, n_grp = vt_ref.shape[0], vt_ref.shape[1]
    blk = MOBA_BLOCK
    grp = ATTN_GROUP
    gk = grp * blk
    nb = n_grp * grp
    dh = HEAD_DIM

    @pl.when(i == 0)
    def _():
        col = lax.broadcasted_iota(jnp.int32, (blk, dh), 1)
        for h in range(n_heads):
            for c in range(nb):
                rows = slice(c * blk, (c + 1) * blk)
                lane0 = (c % grp) * blk
                vt = v_ref[rows, h * dh:(h + 1) * dh].T.astype(BF16)
                vt_ref[h, c // grp, :, lane0:lane0 + blk] = vt
                vtb_ref[h, c] = vt
                kx_ref[h, rows, 0:dh] = kb_ref[rows, h * dh:(h + 1) * dh]
                kx_ref[h, rows, dh:2 * dh] = jnp.where(col == c, 1.0, 0.0).astype(BF16)

    def scores(h, g, q_ext):
        kg = kx_ref[h, pl.ds(pl.multiple_of(g * gk, gk), gk), :]
        return _dot_nt(kg, q_ext)

    def update(h, g, slot, state):
        m, l, acc = state
        s = s_ref[h, slot]
        m_new = jnp.maximum(m, jnp.max(s, axis=0, keepdims=True))
        p = jnp.exp2(s - m_new)
        alpha = jnp.exp2(m - m_new)
        l = alpha * l + jnp.sum(p, axis=0, keepdims=True)
        acc = acc * alpha + _dot(vt_ref[h, g], p.astype(BF16))
        return m_new, l, acc

    qbs, gates, s0s, q_exts, state = [], [], [], [], []
    for h in range(n_heads):
        q = q_ref[:, h * dh:(h + 1) * dh]
        qbs.append((q * EXP2_SCALE).astype(BF16))
        q_hi, q_mid, _ = _split3(q)
        k_hi, k_mid, _ = _split3(km_ref[:, h * dh:(h + 1) * dh])
        gates.append(_dot_nt(k_hi, q_hi) + (_dot_nt(k_hi, q_mid) + _dot_nt(k_mid, q_hi)))

    for h in range(n_heads):
        s0s.append(_dot_nt(kb_ref[0:gk, h * dh:(h + 1) * dh], qbs[h]))
        k_own = kb_ref[pl.ds(pl.multiple_of(i * blk, blk), blk), h * dh:(h + 1) * dh]
        s = _dot_nt(k_own, qbs[h])
        key_pos = lax.broadcasted_iota(jnp.int32, s.shape, 0)
        qry_pos = lax.broadcasted_iota(jnp.int32, s.shape, 1)
        sd_ref[h] = jnp.where(key_pos <= qry_pos, s, NEG_BIG)

    for h in range(n_heads):
        bias = _top_blocks_bias(gates[h], i, nb)
        s_ref[h, 0] = (s0s[h].reshape(grp, blk, blk) + bias[0:grp][:, None, :]).reshape(gk, blk)
        bias = jnp.concatenate([bias, jnp.zeros((dh - nb, blk), F32)], axis=0)
        q_exts.append(jnp.concatenate([qbs[h], bias.T.astype(BF16)], axis=1))
        state.append((jnp.full((1, blk), NEG_BIG, F32), jnp.zeros((1, blk), F32),
                      jnp.zeros((dh, blk), F32)))

    n_groups = lax.shift_right_logical(i + (grp - 1), ATTN_GROUP_LOG2)
    n_pairs = lax.shift_right_logical(n_groups, 1)

    def body(u, carry):
        g0 = 2 * u
        for h in range(n_heads):
            s_ref[h, 1] = scores(h, g0 + 1, q_exts[h])
        carry = tuple(update(h, g0, 0, carry[h]) for h in range(n_heads))
        for h in range(n_heads):
            s_ref[h, 0] = scores(h, jnp.minimum(g0 + 2, n_grp - 1), q_exts[h])
        return tuple(update(h, g0 + 1, 1, carry[h]) for h in range(n_heads))

    state = lax.fori_loop(0, n_pairs, body, tuple(state))

    def finish(with_group):
        for h in range(n_heads):
            m, l, acc = state[h]
            sd = sd_ref[h]
            m_new = jnp.maximum(m, jnp.max(sd, axis=0, keepdims=True))
            if with_group:
                s = s_ref[h, 0]
                m_new = jnp.maximum(m_new, jnp.max(s, axis=0, keepdims=True))
            pd = jnp.exp2(sd - m_new)
            alpha = jnp.exp2(m - m_new)
            l = alpha * l + jnp.sum(pd, axis=0, keepdims=True)
            acc = acc * alpha + _dot(vtb_ref[h, i], pd.astype(BF16))
            if with_group:
                p = jnp.exp2(s - m_new)
                l = l + jnp.sum(p, axis=0, keepdims=True)
                acc = acc + _dot(vt_ref[h, 2 * n_pairs], p.astype(BF16))
            o_ref[:, h * dh:(h + 1) * dh] = (acc / l).T.astype(BF16)

    odd = 2 * n_pairs < n_groups
    pl.when(odd)(lambda: finish(True))
    pl.when(jnp.logical_not(odd))(lambda: finish(False))


def _attn_prompt(q, kb, v, km):
    t = q.shape[0]
    nb = t // MOBA_BLOCK
    hp = ATTN_HEADS_PER_STEP
    assert nb % ATTN_GROUP == 0 and ATTN_HEADS % hp == 0
    w = hp * HEAD_DIM
    return pl.pallas_call(
        _attn_prompt_kernel,
        grid=(ATTN_HEADS // hp, nb),
        in_specs=[pl.BlockSpec((MOBA_BLOCK, w), lambda h, i: (i, h)),
                  pl.BlockSpec((t, w), lambda h, i: (0, h)),
                  pl.BlockSpec((t, w), lambda h, i: (0, h)),
                  pl.BlockSpec((nb, w), lambda h, i: (0, h))],
        out_specs=pl.BlockSpec((MOBA_BLOCK, w), lambda h, i: (i, h)),
        out_shape=jax.ShapeDtypeStruct((t, ATTN_WIDTH), BF16),
        scratch_shapes=[pltpu.VMEM((hp, t, 2 * HEAD_DIM), BF16),
                        pltpu.VMEM((hp, nb // ATTN_GROUP, HEAD_DIM, ATTN_GROUP * MOBA_BLOCK), BF16),
                        pltpu.VMEM((hp, nb, HEAD_DIM, MOBA_BLOCK), BF16),
                        pltpu.VMEM((hp, 2, ATTN_GROUP * MOBA_BLOCK, MOBA_BLOCK), F32),
                        pltpu.VMEM((hp, MOBA_BLOCK, MOBA_BLOCK), F32)],
        compiler_params=_params("arbitrary", "arbitrary"),
        name="attn_prompt",
    )(q, kb, v, km)


def _attn_sample_kernel(n_pages, pt_ref, q_ref, kn_ref, vn_ref, *refs):
    k_refs = refs[:n_pages]
    cv_ref, o_ref, s_ref, vbuf_ref, stat_ref, pick_ref, sem = refs[n_pages:]
    t = pl.program_id(0)
    n_seq = pl.num_programs(0) - 1
    page = k_refs[0].shape[0]
    pages_per_blk = MOBA_BLOCK // page
    n_blk = n_pages // pages_per_blk
    n_heads, sub = ATTN_HEADS, V7X_SUBLANES
    head_picks = [(h, r) for h in range(n_heads) for r in range(MOBA_TOPK)]

    ch = SAMPLE_CHUNK
    chunks = [(pg, c * ch) for pg in range(n_pages) for c in range(page // ch)]

    def slab_copy(slot, seq, h, r, blk_idx, part):
        page_id = pt_ref[seq, blk_idx * pages_per_blk + part]
        return pltpu.make_async_copy(cv_ref.at[0, page_id, :, h, :],
                                     vbuf_ref.at[slot, h, r, pl.ds(part * page, page), :],
                                     sem.at[slot])

    @pl.when(t < n_seq)
    def _():
        slot = jnp.bitwise_and(t, 1)
        q = q_ref[...]
        qe = q * EXP2_SCALE

        def score(k):
            return jnp.broadcast_to(jnp.sum(k * qe, axis=-1, keepdims=True), k.shape)

        ksum = [None] * n_blk
        smax = [None] * n_blk
        for pg, r0 in chunks:
            b = pg // pages_per_blk
            k = k_refs[pg][r0:r0 + ch]
            s = score(k)
            row0 = (pg * page + r0) * sub
            s_ref[slot, row0:row0 + ch * sub, :] = s.reshape(ch * sub, HEAD_DIM)
            ks, cm = jnp.sum(k, axis=0), jnp.max(s, axis=0)
            ksum[b] = ks if ksum[b] is None else ksum[b] + ks
            smax[b] = cm if smax[b] is None else jnp.maximum(smax[b], cm)

        kmean = jnp.concatenate([ks[None] for ks in ksum], axis=0) * (1.0 / MOBA_BLOCK)
        gate = jnp.sum(kmean * q[None], axis=-1, keepdims=True)
        bias, chosen = _top_blocks(gate, n_blk, n_blk)
        for n, (h, r) in enumerate(head_picks):
            blk_idx = jnp.minimum(chosen[r][0, h, 0], n_blk - 1)
            pick_ref[slot, n] = blk_idx
            for part in range(pages_per_blk):
                slab_copy(slot, t, h, r, blk_idx, part).start()

        s_new = score(kn_ref[...])
        m = s_new
        for b in range(n_blk):
            m = jnp.maximum(m, smax[b] + bias[b])
        stat_ref[slot, 0] = m
        stat_ref[slot, 1] = s_new

    @pl.when(t > 0)
    def _():
        slot = jnp.bitwise_and(t + 1, 1)
        picks = [(h, r, pick_ref[slot, n]) for n, (h, r) in enumerate(head_picks)]
        for h, r, blk_idx in picks:
            for part in range(pages_per_blk):
                slab_copy(slot, t - 1, h, r, blk_idx, part).wait()

        m, s_new = stat_ref[slot, 0], stat_ref[slot, 1]
        fold = lambda x: jnp.sum(x.reshape(MOBA_BLOCK // sub, sub, HEAD_DIM), axis=0)
        for h in range(n_heads):
            m_h = m[h:h + 1, :]
            p_new = jnp.exp2(s_new[h:h + 1, :] - m_h)
            l_h = jnp.zeros((sub, HEAD_DIM), F32)
            acc_h = jnp.zeros((sub, HEAD_DIM), F32)
            for hh, r, blk_idx in picks:
                if hh != h:
                    continue
                rows = pl.ds(blk_idx * (MOBA_BLOCK * sub) + h, MOBA_BLOCK, stride=sub)
                p = jnp.exp2(s_ref[slot, rows, :] - m_h)
                l_h = l_h + fold(p)
                acc_h = acc_h + fold(p * vbuf_ref[slot, h, r])
            l_row = jnp.sum(l_h, axis=0, keepdims=True) + p_new
            acc_row = jnp.sum(acc_h, axis=0, keepdims=True) + p_new * vn_ref[h:h + 1, :]
            o_ref[h:h + 1, :] = acc_row / l_row


def _attn_sample(q, k_new, v_new, cache_k, cache_v, page_table):
    n_seq, n_pages = page_table.shape
    page = cache_k.shape[2]
    assert MOBA_BLOCK % page == 0 and (n_pages * page) % MOBA_BLOCK == 0
    assert n_pages * page // MOBA_BLOCK >= MOBA_TOPK and ATTN_HEADS == V7X_SUBLANES
    lanes = (ATTN_HEADS, HEAD_DIM)
    last = n_seq - 1
    key_row = pl.BlockSpec((None,) + lanes, lambda t, pt: (jnp.minimum(t, last), 0, 0))
    val_row = pl.BlockSpec((None,) + lanes, lambda t, pt: (jnp.maximum(t - 1, 0), 0, 0))
    page_spec = lambda pg: pl.BlockSpec((None, None, page) + lanes,
                                        lambda t, pt: (0, pt[jnp.minimum(t, last), pg], 0, 0, 0))
    return pl.pallas_call(
        functools.partial(_attn_sample_kernel, n_pages),
        grid_spec=pltpu.PrefetchScalarGridSpec(
            num_scalar_prefetch=1,
            grid=(n_seq + 1,),
            in_specs=([key_row, key_row, val_row] + [page_spec(pg) for pg in range(n_pages)]
                      + [pl.BlockSpec(memory_space=pl.ANY)]),
            out_specs=val_row,
            scratch_shapes=[pltpu.VMEM((2, n_pages * page * ATTN_HEADS, HEAD_DIM), F32),
                            pltpu.VMEM((2, ATTN_HEADS, MOBA_TOPK, MOBA_BLOCK, HEAD_DIM), F32),
                            pltpu.VMEM((2, 2) + lanes, F32),
                            pltpu.SMEM((2, ATTN_HEADS * MOBA_TOPK), jnp.int32),
                            pltpu.SemaphoreType.DMA((2,))],
        ),
        out_shape=jax.ShapeDtypeStruct((n_seq,) + lanes, F32),
        compiler_params=_params("arbitrary"),
        name="attn_sample",
    )(page_table, q, k_new, v_new, *([cache_k] * n_pages), cache_v)


def _out_proj_kernel(x_ref, a_ref, c_ref, w_ref, g_ref, y_ref):
    mix = _dot(a_ref[...], w_ref[0:ATTN_WIDTH, :]) + _dot(c_ref[...], w_ref[ATTN_WIDTH:, :])
    y_ref[...] = x_ref[...] + _rms(mix, g_ref[...])


def _out_proj(x, attn, conv, w_out, g_post, *, tm):
    m, d = x.shape
    row = lambda width: pl.BlockSpec((tm, width), lambda i: (i, 0))
    return pl.pallas_call(
        _out_proj_kernel,
        grid=(m // tm,),
        in_specs=[row(d), row(ATTN_WIDTH), row(CONV_WIDTH),
                  pl.BlockSpec((d, d), lambda i: (0, 0)), pl.BlockSpec((1, d), lambda i: (0, 0))],
        out_specs=row(d),
        out_shape=jax.ShapeDtypeStruct((m, d), F32),
        compiler_params=_params("arbitrary"),
        name="out_proj",
    )(x, attn, conv, w_out, g_post)


def _ple_kernel(x_ref, hn_ref, p_ref, wg_ref, wp_ref, y_ref):
    gate = _dot(hn_ref[...], wg_ref[...])
    emb = _dot(p_ref[...].astype(BF16), wp_ref[...])
    y_ref[...] = x_ref[...] + (1.0 / (1.0 + jnp.exp(-gate))) * emb


def _ple(x, hn, p, w_gate, w_proj, *, tm, tn):
    m, d = x.shape
    pd = p.shape[1]
    return pl.pallas_call(
        _ple_kernel,
        grid=(m // tm, d // tn),
        in_specs=[pl.BlockSpec((tm, tn), lambda i, j: (i, j)),
                  pl.BlockSpec((tm, d), lambda i, j: (i, 0)),
                  pl.BlockSpec((tm, pd), lambda i, j: (i, 0)),
                  pl.BlockSpec((d, tn), lambda i, j: (0, j)),
                  pl.BlockSpec((pd, tn), lambda i, j: (0, j))],
        out_specs=pl.BlockSpec((tm, tn), lambda i, j: (i, j)),
        out_shape=jax.ShapeDtypeStruct((m, d), F32),
        compiler_params=_params("arbitrary", "arbitrary"),
        name="ple",
    )(x, hn, p, w_gate, w_proj)


def _tiles(m):
    tm = min(m, 512)
    return dict(tm=tm, tf=512, tn=ATTN_WIDTH, tp=D_MODEL)


def kernel(x_prompt, x_sample, cache_k, cache_v, state_conv, page_table, p_prompt, p_sample, w_in, w_out, w_conv, w_ffn1_gate, w_ffn1_up, w_ffn1_down, w_ffn2_gate, w_ffn2_up, w_ffn2_down, w_ple_gate, w_ple_proj, g_ffn1_pre, g_ffn1_post, g_mix_pre, g_mix_post, g_ffn2_pre, g_ffn2_post, g_ple):
    depth = w_in.shape[0]
    assert depth == 1 and x_prompt.shape[0] == 1 and x_sample.shape[1] == 1
    n_seq = x_sample.shape[0]
    seq = x_prompt.shape[1]
    n_phys, page = cache_k.shape[1], cache_k.shape[2]

    bf = lambda w: w[0].astype(BF16)
    w1g, w1u, w1d = bf(w_ffn1_gate), bf(w_ffn1_up), bf(w_ffn1_down)
    wc = w_conv[0]

    tp, ts = _tiles(seq), _tiles(n_seq)
    heads = (ATTN_HEADS, HEAD_DIM)

    later = (w_ffn2_gate[0], w_ffn2_up[0], w_ffn2_down[0], w_in[0], w_out[0],
             w_ple_gate[0], w_ple_proj[0])
    xp1, hp, xs1, hs, w2g, w2u, w2d, w_in_b, w_out_b, w_pg, w_pp = _ffn(
        x_prompt[0], x_sample[:, 0, :], g_ffn1_pre, g_ffn1_post, g_mix_pre,
        w1g, w1u, w1d, tm=tp["tm"], tf=tp["tf"], cast=later)

    qp, kp, vp, kbp, kmp = _qkv_proj(hp, w_in_b, tm=tp["tm"], with_block_means=True)
    conv_p, tail_p = _conv_prompt(hp, w_in_b, wc, tm=512, tc=512)
    attn_p = _attn_prompt(qp, kbp, vp, kmp.reshape(seq // MOBA_BLOCK, ATTN_WIDTH))
    xp2 = _out_proj(xp1, attn_p, conv_p, w_out_b, g_mix_post, tm=tp["tm"])

    qs, ks, vs = _qkv_proj(hs, w_in_b, tm=ts["tm"], with_block_means=False)
    state_t = jnp.transpose(state_conv[0], (1, 0, 2))
    conv_s, nstate_t = _conv_sample(hs, w_in_b, wc, state_t, tc=512)
    qs3, ks3, vs3 = (a.reshape(n_seq, *heads) for a in (qs, ks, vs))
    attn_s = _attn_sample(qs3, ks3, vs3, cache_k, cache_v, page_table)
    attn_s = attn_s.reshape(n_seq, ATTN_WIDTH).astype(BF16)
    xs2 = _out_proj(xs1, attn_s, conv_s, w_out_b, g_mix_post, tm=ts["tm"])

    xp3, hnp, xs3, hns = _ffn(xp2, xs2, g_ffn2_pre, g_ffn2_post, g_ple, w2g, w2u, w2d,
                              tm=tp["tm"], tf=tp["tf"])
    y_prompt = _ple(xp3, hnp, p_prompt[0, 0], w_pg, w_pp, tm=tp["tm"], tn=tp["tp"])
    y_sample = _ple(xs3, hns, p_sample[0, :, 0, :], w_pg, w_pp, tm=ts["tm"], tn=ts["tp"])

    return (y_prompt[None],
            y_sample[:, None, :],
            kp.reshape(1, 1, seq, *heads),
            vp.reshape(1, 1, seq, *heads),
            tail_p[V7X_SUBLANES - (CONV_K - 1):][None, None],
            ks.reshape(1, n_seq, 1, *heads),
            vs.reshape(1, n_seq, 1, *heads),
            jnp.transpose(nstate_t, (1, 0, 2))[None])
```

```python
import functools
import math

import jax
import jax.numpy as jnp
from jax import lax
from jax.experimental import pallas as pl
from jax.experimental.pallas import tpu as pltpu

F32 = jnp.float32
BF16 = jnp.bfloat16

D_MODEL = 2048
HEAD_DIM = 128
ATTN_HEADS = 8
ATTN_WIDTH = ATTN_HEADS * HEAD_DIM
CONV_WIDTH = D_MODEL - ATTN_WIDTH
CONV_K = 3
MOBA_BLOCK = 256
MOBA_TOPK = 3
RMS_EPS = 1e-6

V7X_SUBLANES = 8
V7X_LANES = 128
V7X_VMEM_LIMIT_BYTES = 56 * 1024 * 1024

FFN_EDGE_CHUNK = 256
ATTN_GROUP_LOG2 = 2
ATTN_GROUP = 1 << ATTN_GROUP_LOG2
ATTN_HEADS_PER_STEP = 2
SAMPLE_CHUNK = 8

NEG_BIG = -1e30
EXP2_SCALE = (HEAD_DIM ** -0.5) * math.log2(math.e)


def _params(*semantics):
    return pltpu.CompilerParams(dimension_semantics=semantics,
                                vmem_limit_bytes=V7X_VMEM_LIMIT_BYTES)


def _rms(x, g):
    var = jnp.mean(x * x, axis=-1, keepdims=True)
    return x * lax.rsqrt(var + RMS_EPS) * g


def _dot(a, b):
    return jnp.dot(a, b, preferred_element_type=F32)


def _dot_nt(a, b):
    return lax.dot_general(a, b, (((1,), (1,)), ((), ())), preferred_element_type=F32)


def _split3(x):
    hi = x.astype(BF16)
    r1 = x - hi.astype(F32)
    mid = r1.astype(BF16)
    lo = (r1 - mid.astype(F32)).astype(BF16)
    return hi, mid, lo


def _ffn_kernel(n_cast, x_ref, xs_ref, gpre_ref, gpost_ref, gnext_ref, wg_ref, wu_ref, wd_ref,
                *refs):
    n = n_cast
    w32_refs, (y_ref, hn_ref, ys_ref, hns_ref) = refs[:n], refs[n:n + 4]
    w16_refs = refs[n + 4:2 * n + 4]
    xn_ref, acc_ref, xns_ref, accs_ref = refs[2 * n + 4:]
    f = pl.program_id(1)

    def rows(x_ref, y_ref, hn_ref, xn_ref, acc_ref, side_casts=False):
        last = pl.num_programs(1) - 1
        n_rows = x_ref.shape[0]
        chunk = min(n_rows, FFN_EDGE_CHUNK)
        chunks = [slice(r, r + chunk) for r in range(0, n_rows, chunk)]

        def casts():
            if side_casts:
                for w32_ref, w16_ref in zip(w32_refs, w16_refs):
                    w16_ref[...] = w32_ref[...].astype(BF16)

        def mlp(xn):
            g = _dot(xn, wg_ref[...])
            u = _dot(xn, wu_ref[...])
            a = (g * (1.0 / (1.0 + jnp.exp(-g))) * u).astype(BF16)
            return _dot(a, wd_ref[...])

        @pl.when(f == 0)
        def _():
            casts()
            for c in chunks:
                xn = _rms(x_ref[c, :], gpre_ref[...]).astype(BF16)
                xn_ref[c, :] = xn
                acc_ref[c, :] = mlp(xn)

        @pl.when(jnp.logical_and(f > 0, f < last))
        def _():
            casts()
            acc_ref[...] += mlp(xn_ref[...])

        @pl.when(f == last)
        def _():
            casts()
            for c in chunks:
                acc = acc_ref[c, :] + mlp(xn_ref[c, :])
                y = x_ref[c, :] + 0.5 * _rms(acc, gpost_ref[...])
                y_ref[c, :] = y
                hn_ref[c, :] = _rms(y, gnext_ref[...]).astype(BF16)

    rows(x_ref, y_ref, hn_ref, xn_ref, acc_ref, side_casts=True)

    @pl.when(pl.program_id(0) == pl.num_programs(0) - 1)
    def _():
        rows(xs_ref, ys_ref, hns_ref, xns_ref, accs_ref)


def _cast_tiling(w, n_row_blocks, n_steps):
    rows, cols = w.shape
    steps = max(s for s in range(1, n_steps + 1)
                if cols % s == 0 and (cols // s) % V7X_LANES == 0)
    return pl.BlockSpec((rows // n_row_blocks, cols // steps),
                        lambda i, f: (i, jnp.minimum(f, steps - 1)))


def _ffn(x, xs, g_pre, g_post, g_next, wg, wu, wd, *, tm, cast_tiled=(), cast=()):
    m, d = x.shape
    ms = xs.shape[0]
    n_f, _, tf = wg.shape
    grid = (m // tm, n_f)
    row = pl.BlockSpec((tm, d), lambda i, f: (i, 0))
    small = pl.BlockSpec((ms, d), lambda i, f: (0, 0))
    vec = pl.BlockSpec((1, d), lambda i, f: (0, 0))
    wtile = pl.BlockSpec((None, d, tf), lambda i, f: (f, 0, 0))
    rows_per_tile = d // grid[0]
    side_in = ([pl.BlockSpec((rows_per_tile, tf), lambda i, f: (i, f)) for _ in cast_tiled]
               + [_cast_tiling(w, *grid) for w in cast])
    side_out = ([pl.BlockSpec((None, rows_per_tile, tf), lambda i, f: (f, i, 0)) for _ in cast_tiled]
                + side_in[len(cast_tiled):])
    side_shapes = ([jax.ShapeDtypeStruct((n_f, d, tf), BF16) for _ in cast_tiled]
                   + [jax.ShapeDtypeStruct(w.shape, BF16) for w in cast])
    return pl.pallas_call(
        functools.partial(_ffn_kernel, len(side_in)),
        grid=grid,
        in_specs=[row, small, vec, vec, vec, wtile, wtile,
                  pl.BlockSpec((tf, d), lambda i, f: (f, 0)), *side_in],
        out_specs=[row, row, small, small, *side_out],
        out_shape=[jax.ShapeDtypeStruct((m, d), F32), jax.ShapeDtypeStruct((m, d), BF16),
                   jax.ShapeDtypeStruct((ms, d), F32), jax.ShapeDtypeStruct((ms, d), BF16),
                   *side_shapes],
        scratch_shapes=[pltpu.VMEM((tm, d), BF16), pltpu.VMEM((tm, d), F32),
                        pltpu.VMEM((ms, d), BF16), pltpu.VMEM((ms, d), F32)],
        compiler_params=_params("arbitrary", "arbitrary"),
        name="ffn",
    )(x, xs, g_pre, g_post, g_next, wg, wu, wd, *cast_tiled, *cast)


def _qkv_kernel(h_ref, wq_ref, wk_ref, wv_ref, q_ref, k_ref, v_ref, *extra):
    h = h_ref[...]
    q_ref[...] = _dot(h, wq_ref[...])
    k = _dot(h, wk_ref[...])
    k_ref[...] = k
    v_ref[...] = _dot(h, wv_ref[...])
    if extra:
        kb_ref, km_ref = extra
        kb_ref[...] = k.astype(BF16)
        blocks, width = km_ref.shape
        km_ref[...] = jnp.sum(k.reshape(blocks, MOBA_BLOCK, width), axis=1) * (1.0 / MOBA_BLOCK)


def _qkv_proj(hb, w_in, *, tm, with_block_means):
    m, d = hb.shape
    tn = ATTN_WIDTH
    wspec = lambda part: pl.BlockSpec((d, tn), lambda i: (0, part))
    ospec = pl.BlockSpec((tm, tn), lambda i: (i, 0))
    f32o = jax.ShapeDtypeStruct((m, tn), F32)
    out_specs, out_shape = [ospec] * 3, [f32o] * 3
    if with_block_means:
        bpt = tm // MOBA_BLOCK
        out_specs = out_specs + [ospec, pl.BlockSpec((None, bpt, tn), lambda i: (i, 0, 0))]
        out_shape = out_shape + [jax.ShapeDtypeStruct((m, tn), BF16),
                                 jax.ShapeDtypeStruct((m // tm, bpt, tn), F32)]
    return pl.pallas_call(
        _qkv_kernel,
        grid=(m // tm,),
        in_specs=[pl.BlockSpec((tm, d), lambda i: (i, 0)), wspec(0), wspec(1), wspec(2)],
        out_specs=out_specs,
        out_shape=out_shape,
        compiler_params=_params("arbitrary"),
        name="qkv_proj",
    )(hb, w_in, w_in, w_in)


def _conv_prompt_kernel(h_ref, wb_ref, wc_ref, wu_ref, wconv_ref, conv_ref, tail_ref, zbuf_ref):
    i = pl.program_id(1)
    tm = h_ref.shape[0]
    halo = V7X_SUBLANES

    @pl.when(i == 0)
    def _():
        zbuf_ref[0:halo, :] = jnp.zeros((halo, zbuf_ref.shape[1]), F32)

    h = h_ref[...]
    z = _dot(h, wc_ref[...]) * _dot(h, wu_ref[...])
    zbuf_ref[halo:halo + tm, :] = z
    w = wconv_ref[...]
    y = zbuf_ref[halo:halo + tm, :] * w[CONV_K - 1:CONV_K, :]
    for j in range(CONV_K - 1):
        off = halo - (CONV_K - 1) + j
        y = y + zbuf_ref[off:off + tm, :] * w[j:j + 1, :]
    conv_ref[...] = (_dot(h, wb_ref[...]) * y).astype(BF16)
    last = zbuf_ref[tm:tm + halo, :]
    zbuf_ref[0:halo, :] = last
    tail_ref[...] = last


def _conv_prompt(hb, w_in, w_conv, *, tm, tc):
    m, d = hb.shape
    nj = CONV_WIDTH // tc
    base = 3 * ATTN_WIDTH // tc
    wspec = lambda part: pl.BlockSpec((d, tc), lambda j, i: (0, base + part * nj + j))
    return pl.pallas_call(
        _conv_prompt_kernel,
        grid=(nj, m // tm),
        in_specs=[pl.BlockSpec((tm, d), lambda j, i: (i, 0)), wspec(0), wspec(1), wspec(2),
                  pl.BlockSpec((CONV_K, tc), lambda j, i: (0, j))],
        out_specs=[pl.BlockSpec((tm, tc), lambda j, i: (i, j)),
                   pl.BlockSpec((V7X_SUBLANES, tc), lambda j, i: (0, j))],
        out_shape=[jax.ShapeDtypeStruct((m, CONV_WIDTH), BF16),
                   jax.ShapeDtypeStruct((V7X_SUBLANES, CONV_WIDTH), F32)],
        scratch_shapes=[pltpu.VMEM((tm + V7X_SUBLANES, tc), F32)],
        compiler_params=_params("arbitrary", "arbitrary"),
        name="conv_prompt",
    )(hb, w_in, w_in, w_in, w_conv)


def _conv_sample_kernel(h_ref, wb_ref, wc_ref, wu_ref, wconv_ref, st_ref, conv_ref, nst_ref):
    h = h_ref[...]
    z = _dot(h, wc_ref[...]) * _dot(h, wu_ref[...])
    w = wconv_ref[...]
    y = z * w[CONV_K - 1:CONV_K, :]
    for j in range(CONV_K - 1):
        y = y + st_ref[j] * w[j:j + 1, :]
    conv_ref[...] = (_dot(h, wb_ref[...]) * y).astype(BF16)
    for j in range(CONV_K - 2):
        nst_ref[j] = st_ref[j + 1]
    nst_ref[CONV_K - 2] = z


def _conv_sample(hb, w_in, w_conv, state_t, *, tc):
    m, d = hb.shape
    nj = CONV_WIDTH // tc
    base = 3 * ATTN_WIDTH // tc
    wspec = lambda part: pl.BlockSpec((d, tc), lambda j: (0, base + part * nj + j))
    sspec = pl.BlockSpec((CONV_K - 1, m, tc), lambda j: (0, 0, j))
    return pl.pallas_call(
        _conv_sample_kernel,
        grid=(nj,),
        in_specs=[pl.BlockSpec((m, d), lambda j: (0, 0)), wspec(0), wspec(1), wspec(2),
                  pl.BlockSpec((CONV_K, tc), lambda j: (0, j)), sspec],
        out_specs=[pl.BlockSpec((m, tc), lambda j: (0, j)), sspec],
        out_shape=[jax.ShapeDtypeStruct((m, CONV_WIDTH), BF16),
                   jax.ShapeDtypeStruct((CONV_K - 1, m, CONV_WIDTH), F32)],
        compiler_params=_params("arbitrary"),
        name="conv_sample",
    )(hb, w_in, w_in, w_in, w_conv, state_t)


def _top_blocks(gate, n_past, axis_len):
    blk = lax.broadcasted_iota(jnp.int32, gate.shape, 0)
    past = blk < n_past
    g = jnp.where(past, gate, -jnp.inf)
    keep = jnp.zeros(gate.shape, F32)
    chosen = []
    for _ in range(MOBA_TOPK):
        mx = jnp.max(g, axis=0, keepdims=True)
        idx = jnp.min(jnp.where(g == mx, blk, axis_len), axis=0, keepdims=True)
        pick = blk == idx
        keep = jnp.where(pick, jnp.where(past, 1.0, keep), keep)
        g = jnp.where(pick, -jnp.inf, g)
        chosen.append(idx)
    return jnp.where(keep > 0.0, 0.0, NEG_BIG), chosen


def _top_blocks_bias(gate, n_past, axis_len):
    return _top_blocks(gate, n_past, axis_len)[0]


def _attn_prompt_kernel(q_ref, kb_ref, v_ref, km_ref, o_ref, kx_ref, vt_ref, vtb_ref, s_ref, sd_ref):
    i = pl.program_id(1)
    n_heads, n_grp = vt_ref.shape[0], vt_ref.shape[1]
    blk = MOBA_BLOCK
    grp = ATTN_GROUP
    gk = grp * blk
    nb = n_grp * grp
    dh = HEAD_DIM

    @pl.when(i == 0)
    def _():
        col = lax.broadcasted_iota(jnp.int32, (blk, dh), 1)
        for h in range(n_heads):
            for c in range(nb):
                rows = slice(c * blk, (c + 1) * blk)
                lane0 = (c % grp) * blk
                vt = v_ref[rows, h * dh:(h + 1) * dh].T.astype(BF16)
                vt_ref[h, c // grp, :, lane0:lane0 + blk] = vt
                vtb_ref[h, c] = vt
                kx_ref[h, rows, 0:dh] = kb_ref[rows, h * dh:(h + 1) * dh]
                kx_ref[h, rows, dh:2 * dh] = jnp.where(col == c, 1.0, 0.0).astype(BF16)

    def scores(h, g, q_ext):
        kg = kx_ref[h, pl.ds(pl.multiple_of(g * gk, gk), gk), :]
        return _dot_nt(kg, q_ext)

    def update(h, g, slot, state):
        m, l, acc = state
        s = s_ref[h, slot]
        m_new = jnp.maximum(m, jnp.max(s, axis=0, keepdims=True))
        p = jnp.exp2(s - m_new)
        alpha = jnp.exp2(m - m_new)
        l = alpha * l + jnp.sum(p, axis=0, keepdims=True)
        acc = acc * alpha + _dot(vt_ref[h, g], p.astype(BF16))
        return m_new, l, acc

    qbs, gates, s0s, q_exts, state = [], [], [], [], []
    for h in range(n_heads):
        q = q_ref[:, h * dh:(h + 1) * dh]
        qbs.append((q * EXP2_SCALE).astype(BF16))
        q_hi, q_mid, _ = _split3(q)
        k_hi, k_mid, _ = _split3(km_ref[:, h * dh:(h + 1) * dh])
        gates.append(_dot_nt(k_hi, q_hi) + (_dot_nt(k_hi, q_mid) + _dot_nt(k_mid, q_hi)))

    for h in range(n_heads):
        s0s.append(_dot_nt(kb_ref[0:gk, h * dh:(h + 1) * dh], qbs[h]))
        k_own = kb_ref[pl.ds(pl.multiple_of(i * blk, blk), blk), h * dh:(h + 1) * dh]
        s = _dot_nt(k_own, qbs[h])
        key_pos = lax.broadcasted_iota(jnp.int32, s.shape, 0)
        qry_pos = lax.broadcasted_iota(jnp.int32, s.shape, 1)
        sd_ref[h] = jnp.where(key_pos <= qry_pos, s, NEG_BIG)

    for h in range(n_heads):
        bias = _top_blocks_bias(gates[h], i, nb)
        s_ref[h, 0] = (s0s[h].reshape(grp, blk, blk) + bias[0:grp][:, None, :]).reshape(gk, blk)
        bias = jnp.concatenate([bias, jnp.zeros((dh - nb, blk), F32)], axis=0)
        q_exts.append(jnp.concatenate([qbs[h], bias.T.astype(BF16)], axis=1))
        state.append((jnp.full((1, blk), NEG_BIG, F32), jnp.zeros((1, blk), F32),
                      jnp.zeros((dh, blk), F32)))

    n_groups = lax.shift_right_logical(i + (grp - 1), ATTN_GROUP_LOG2)
    n_pairs = lax.shift_right_logical(n_groups, 1)

    def body(u, carry):
        g0 = 2 * u
        for h in range(n_heads):
            s_ref[h, 1] = scores(h, g0 + 1, q_exts[h])
        carry = tuple(update(h, g0, 0, carry[h]) for h in range(n_heads))
        for h in range(n_heads):
            s_ref[h, 0] = scores(h, jnp.minimum(g0 + 2, n_grp - 1), q_exts[h])
        return tuple(update(h, g0 + 1, 1, carry[h]) for h in range(n_heads))

    state = lax.fori_loop(0, n_pairs, body, tuple(state))

    def finish(with_group):
        for h in range(n_heads):
            m, l, acc = state[h]
            sd = sd_ref[h]
            m_new = jnp.maximum(m, jnp.max(sd, axis=0, keepdims=True))
            if with_group:
                s = s_ref[h, 0]
                m_new = jnp.maximum(m_new, jnp.max(s, axis=0, keepdims=True))
            pd = jnp.exp2(sd - m_new)
            alpha = jnp.exp2(m - m_new)
            l = alpha * l + jnp.sum(pd, axis=0, keepdims=True)
            acc = acc * alpha + _dot(vtb_ref[h, i], pd.astype(BF16))
            if with_group:
                p = jnp.exp2(s - m_new)
                l = l + jnp.sum(p, axis=0, keepdims=True)
                acc = acc + _dot(vt_ref[h, 2 * n_pairs], p.astype(BF16))
            o_ref[:, h * dh:(h + 1) * dh] = (acc / l).T.astype(BF16)

    odd = 2 * n_pairs < n_groups
    pl.when(odd)(lambda: finish(True))
    pl.when(jnp.logical_not(odd))(lambda: finish(False))


def _attn_prompt(q, kb, v, km):
    t = q.shape[0]
    nb = t // MOBA_BLOCK
    hp = ATTN_HEADS_PER_STEP
    assert nb % ATTN_GROUP == 0 and ATTN_HEADS % hp == 0
    w = hp * HEAD_DIM
    return pl.pallas_call(
        _attn_prompt_kernel,
        grid=(ATTN_HEADS // hp, nb),
        in_specs=[pl.BlockSpec((MOBA_BLOCK, w), lambda h, i: (i, h)),
                  pl.BlockSpec((t, w), lambda h, i: (0, h)),
                  pl.BlockSpec((t, w), lambda h, i: (0, h)),
                  pl.BlockSpec((nb, w), lambda h, i: (0, h))],
        out_specs=pl.BlockSpec((MOBA_BLOCK, w), lambda h, i: (i, h)),
        out_shape=jax.ShapeDtypeStruct((t, ATTN_WIDTH), BF16),
        scratch_shapes=[pltpu.VMEM((hp, t, 2 * HEAD_DIM), BF16),
                        pltpu.VMEM((hp, nb // ATTN_GROUP, HEAD_DIM, ATTN_GROUP * MOBA_BLOCK), BF16),
                        pltpu.VMEM((hp, nb, HEAD_DIM, MOBA_BLOCK), BF16),
                        pltpu.VMEM((hp, 2, ATTN_GROUP * MOBA_BLOCK, MOBA_BLOCK), F32),
                        pltpu.VMEM((hp, MOBA_BLOCK, MOBA_BLOCK), F32)],
        compiler_params=_params("arbitrary", "arbitrary"),
        name="attn_prompt",
    )(q, kb, v, km)


def _attn_sample_kernel(n_pages, pt_ref, q_ref, kn_ref, vn_ref, *refs):
    k_refs = refs[:n_pages]
    cv_ref, o_ref, s_ref, vbuf_ref, stat_ref, pick_ref, sem = refs[n_pages:]
    t = pl.program_id(0)
    n_seq = pl.num_programs(0) - 1
    page = k_refs[0].shape[0]
    pages_per_blk = MOBA_BLOCK // page
    n_blk = n_pages // pages_per_blk
    n_heads, sub = ATTN_HEADS, V7X_SUBLANES
    head_picks = [(h, r) for h in range(n_heads) for r in range(MOBA_TOPK)]

    ch = SAMPLE_CHUNK
    chunks = [(pg, c * ch) for pg in range(n_pages) for c in range(page // ch)]

    def slab_copy(slot, seq, h, r, blk_idx, part):
        page_id = pt_ref[seq, blk_idx * pages_per_blk + part]
        return pltpu.make_async_copy(cv_ref.at[0, page_id, :, h, :],
                                     vbuf_ref.at[slot, h, r, pl.ds(part * page, page), :],
                                     sem.at[slot])

    @pl.when(t < n_seq)
    def _():
        slot = jnp.bitwise_and(t, 1)
        q = q_ref[...]
        qe = q * EXP2_SCALE

        def score(k):
            return jnp.broadcast_to(jnp.sum(k * qe, axis=-1, keepdims=True), k.shape)

        ksum = [None] * n_blk
        smax = [None] * n_blk
        for pg, r0 in chunks:
            b = pg // pages_per_blk
            k = k_refs[pg][r0:r0 + ch]
            s = score(k)
            row0 = (pg * page + r0) * sub
            s_ref[slot, row0:row0 + ch * sub, :] = s.reshape(ch * sub, HEAD_DIM)
            ks, cm = jnp.sum(k, axis=0), jnp.max(s, axis=0)
            ksum[b] = ks if ksum[b] is None else ksum[b] + ks
            smax[b] = cm if smax[b] is None else jnp.maximum(smax[b], cm)

        kmean = jnp.concatenate([ks[None] for ks in ksum], axis=0) * (1.0 / MOBA_BLOCK)
        gate = jnp.sum(kmean * q[None], axis=-1, keepdims=True)
        bias, chosen = _top_blocks(gate, n_blk, n_blk)
        for n, (h, r) in enumerate(head_picks):
            blk_idx = jnp.minimum(chosen[r][0, h, 0], n_blk - 1)
            pick_ref[slot, n] = blk_idx
            for part in range(pages_per_blk):
                slab_copy(slot, t, h, r, blk_idx, part).start()

        s_new = score(kn_ref[...])
        m = s_new
        for b in range(n_blk):
            m = jnp.maximum(m, smax[b] + bias[b])
        stat_ref[slot, 0] = m
        stat_ref[slot, 1] = s_new

    @pl.when(t > 0)
    def _():
        slot = jnp.bitwise_and(t + 1, 1)
        picks = [(h, r, pick_ref[slot, n]) for n, (h, r) in enumerate(head_picks)]
        for h, r, blk_idx in picks:
            for part in range(pages_per_blk):
                slab_copy(slot, t - 1, h, r, blk_idx, part).wait()

        m, s_new = stat_ref[slot, 0], stat_ref[slot, 1]
        fold = lambda x: jnp.sum(x.reshape(MOBA_BLOCK // sub, sub, HEAD_DIM), axis=0)
        for h in range(n_heads):
            m_h = m[h:h + 1, :]
            p_new = jnp.exp2(s_new[h:h + 1, :] - m_h)
            l_h = jnp.zeros((sub, HEAD_DIM), F32)
            acc_h = jnp.zeros((sub, HEAD_DIM), F32)
            for hh, r, blk_idx in picks:
                if hh != h:
                    continue
                rows = pl.ds(blk_idx * (MOBA_BLOCK * sub) + h, MOBA_BLOCK, stride=sub)
                p = jnp.exp2(s_ref[slot, rows, :] - m_h)
                l_h = l_h + fold(p)
                acc_h = acc_h + fold(p * vbuf_ref[slot, h, r])
            l_row = jnp.sum(l_h, axis=0, keepdims=True) + p_new
            acc_row = jnp.sum(acc_h, axis=0, keepdims=True) + p_new * vn_ref[h:h + 1, :]
            o_ref[h:h + 1, :] = acc_row / l_row


def _attn_sample(q, k_new, v_new, cache_k, cache_v, page_table):
    n_seq, n_pages = page_table.shape
    page = cache_k.shape[2]
    assert MOBA_BLOCK % page == 0 and (n_pages * page) % MOBA_BLOCK == 0
    assert n_pages * page // MOBA_BLOCK >= MOBA_TOPK and ATTN_HEADS == V7X_SUBLANES
    lanes = (ATTN_HEADS, HEAD_DIM)
    last = n_seq - 1
    key_row = pl.BlockSpec((None,) + lanes, lambda t, pt: (jnp.minimum(t, last), 0, 0))
    val_row = pl.BlockSpec((None,) + lanes, lambda t, pt: (jnp.maximum(t - 1, 0), 0, 0))
    page_spec = lambda pg: pl.BlockSpec((None, None, page) + lanes,
                                        lambda t, pt: (0, pt[jnp.minimum(t, last), pg], 0, 0, 0))
    return pl.pallas_call(
        functools.partial(_attn_sample_kernel, n_pages),
        grid_spec=pltpu.PrefetchScalarGridSpec(
            num_scalar_prefetch=1,
            grid=(n_seq + 1,),
            in_specs=([key_row, key_row, val_row] + [page_spec(pg) for pg in range(n_pages)]
                      + [pl.BlockSpec(memory_space=pl.ANY)]),
            out_specs=val_row,
            scratch_shapes=[pltpu.VMEM((2, n_pages * page * ATTN_HEADS, HEAD_DIM), F32),
                            pltpu.VMEM((2, ATTN_HEADS, MOBA_TOPK, MOBA_BLOCK, HEAD_DIM), F32),
                            pltpu.VMEM((2, 2) + lanes, F32),
                            pltpu.SMEM((2, ATTN_HEADS * MOBA_TOPK), jnp.int32),
                            pltpu.SemaphoreType.DMA((2,))],
        ),
        out_shape=jax.ShapeDtypeStruct((n_seq,) + lanes, F32),
        compiler_params=_params("arbitrary"),
        name="attn_sample",
    )(page_table, q, k_new, v_new, *([cache_k] * n_pages), cache_v)


def _out_proj_kernel(x_ref, a_ref, c_ref, w_ref, g_ref, y_ref):
    mix = _dot(a_ref[...], w_ref[0:ATTN_WIDTH, :]) + _dot(c_ref[...], w_ref[ATTN_WIDTH:, :])
    y_ref[...] = x_ref[...] + _rms(mix, g_ref[...])


def _out_proj(x, attn, conv, w_out, g_post, *, tm):
    m, d = x.shape
    row = lambda width: pl.BlockSpec((tm, width), lambda i: (i, 0))
    return pl.pallas_call(
        _out_proj_kernel,
        grid=(m // tm,),
        in_specs=[row(d), row(ATTN_WIDTH), row(CONV_WIDTH),
                  pl.BlockSpec((d, d), lambda i: (0, 0)), pl.BlockSpec((1, d), lambda i: (0, 0))],
        out_specs=row(d),
        out_shape=jax.ShapeDtypeStruct((m, d), F32),
        compiler_params=_params("arbitrary"),
        name="out_proj",
    )(x, attn, conv, w_out, g_post)


def _ple_kernel(x_ref, hn_ref, p_ref, wg_ref, wp_ref, y_ref):
    gate = _dot(hn_ref[...], wg_ref[...])
    emb = _dot(p_ref[...].astype(BF16), wp_ref[...])
    y_ref[...] = x_ref[...] + (1.0 / (1.0 + jnp.exp(-gate))) * emb


def _ple(x, hn, p, w_gate, w_proj, *, tm, tn):
    m, d = x.shape
    pd = p.shape[1]
    return pl.pallas_call(
        _ple_kernel,
        grid=(m // tm, d // tn),
        in_specs=[pl.BlockSpec((tm, tn), lambda i, j: (i, j)),
                  pl.BlockSpec((tm, d), lambda i, j: (i, 0)),
                  pl.BlockSpec((tm, pd), lambda i, j: (i, 0)),
                  pl.BlockSpec((d, tn), lambda i, j: (0, j)),
                  pl.BlockSpec((pd, tn), lambda i, j: (0, j))],
        out_specs=pl.BlockSpec((tm, tn), lambda i, j: (i, j)),
        out_shape=jax.ShapeDtypeStruct((m, d), F32),
        compiler_params=_params("arbitrary", "arbitrary"),
        name="ple",
    )(x, hn, p, w_gate, w_proj)


def _tiles(m):
    tm = min(m, 512)
    return dict(tm=tm, tf=512, tn=ATTN_WIDTH, tp=D_MODEL)


def kernel(x_prompt, x_sample, cache_k, cache_v, state_conv, page_table, p_prompt, p_sample, w_in, w_out, w_conv, w_ffn1_gate, w_ffn1_up, w_ffn1_down, w_ffn2_gate, w_ffn2_up, w_ffn2_down, w_ple_gate, w_ple_proj, g_ffn1_pre, g_ffn1_post, g_mix_pre, g_mix_post, g_ffn2_pre, g_ffn2_post, g_ple):
    depth = w_in.shape[0]
    assert depth == 1 and x_prompt.shape[0] == 1 and x_sample.shape[1] == 1
    n_seq = x_sample.shape[0]
    seq = x_prompt.shape[1]
    n_phys, page = cache_k.shape[1], cache_k.shape[2]

    tp, ts = _tiles(seq), _tiles(n_seq)
    heads = (ATTN_HEADS, HEAD_DIM)
    wc = w_conv[0]

    def tile_major(w):
        d, d_ff = w.shape
        return jnp.transpose(w.astype(BF16).reshape(d, d_ff // tp["tf"], tp["tf"]), (1, 0, 2))

    w1g, w1u = tile_major(w_ffn1_gate[0]), tile_major(w_ffn1_up[0])
    w1d = w_ffn1_down[0].astype(BF16)

    later = (w_ffn2_down[0], w_in[0], w_out[0], w_ple_gate[0], w_ple_proj[0])
    xp1, hp, xs1, hs, w2g, w2u, w2d, w_in_b, w_out_b, w_pg, w_pp = _ffn(
        x_prompt[0], x_sample[:, 0, :], g_ffn1_pre, g_ffn1_post, g_mix_pre,
        w1g, w1u, w1d, tm=tp["tm"], cast_tiled=(w_ffn2_gate[0], w_ffn2_up[0]), cast=later)

    qp, kp, vp, kbp, kmp = _qkv_proj(hp, w_in_b, tm=tp["tm"], with_block_means=True)
    conv_p, tail_p = _conv_prompt(hp, w_in_b, wc, tm=512, tc=512)
    attn_p = _attn_prompt(qp, kbp, vp, kmp.reshape(seq // MOBA_BLOCK, ATTN_WIDTH))
    xp2 = _out_proj(xp1, attn_p, conv_p, w_out_b, g_mix_post, tm=tp["tm"])

    qs, ks, vs = _qkv_proj(hs, w_in_b, tm=ts["tm"], with_block_means=False)
    state_t = jnp.transpose(state_conv[0], (1, 0, 2))
    conv_s, nstate_t = _conv_sample(hs, w_in_b, wc, state_t, tc=512)
    qs3, ks3, vs3 = (a.reshape(n_seq, *heads) for a in (qs, ks, vs))
    attn_s = _attn_sample(qs3, ks3, vs3, cache_k, cache_v, page_table)
    attn_s = attn_s.reshape(n_seq, ATTN_WIDTH).astype(BF16)
    xs2 = _out_proj(xs1, attn_s, conv_s, w_out_b, g_mix_post, tm=ts["tm"])

    xp3, hnp, xs3, hns = _ffn(xp2, xs2, g_ffn2_pre, g_ffn2_post, g_ple, w2g, w2u, w2d,
                              tm=tp["tm"])
    y_prompt = _ple(xp3, hnp, p_prompt[0, 0], w_pg, w_pp, tm=tp["tm"], tn=tp["tp"])
    y_sample = _ple(xs3, hns, p_sample[0, :, 0, :], w_pg, w_pp, tm=ts["tm"], tn=ts["tp"])

    return (y_prompt[None],
            y_sample[:, None, :],
            kp.reshape(1, 1, seq, *heads),
            vp.reshape(1, 1, seq, *heads),
            tail_p[V7X_SUBLANES - (CONV_K - 1):][None, None],
            ks.reshape(1, n_seq, 1, *heads),
            vs.reshape(1, n_seq, 1, *heads),
            jnp.transpose(nstate_t, (1, 0, 2))[None])
```

```python
import functools
import math

import jax
import jax.numpy as jnp
from jax import lax
from jax.experimental import pallas as pl
from jax.experimental.pallas import tpu as pltpu

F32 = jnp.float32
BF16 = jnp.bfloat16

D_MODEL = 2048
HEAD_DIM = 128
ATTN_HEADS = 8
ATTN_WIDTH = ATTN_HEADS * HEAD_DIM
CONV_WIDTH = D_MODEL - ATTN_WIDTH
CONV_K = 3
MOBA_BLOCK = 256
MOBA_TOPK = 3
RMS_EPS = 1e-6

V7X_SUBLANES = 8
V7X_LANES = 128
V7X_VMEM_LIMIT_BYTES = 56 * 1024 * 1024

FFN_EDGE_CHUNK = 256
ATTN_GROUP_LOG2 = 2
ATTN_GROUP = 1 << ATTN_GROUP_LOG2
ATTN_HEADS_PER_STEP = 2
SAMPLE_CHUNK = 8

NEG_BIG = -1e30
EXP2_SCALE = (HEAD_DIM ** -0.5) * math.log2(math.e)


def _params(*semantics):
    return pltpu.CompilerParams(dimension_semantics=semantics,
                                vmem_limit_bytes=V7X_VMEM_LIMIT_BYTES)


def _rms(x, g):
    var = jnp.mean(x * x, axis=-1, keepdims=True)
    return x * lax.rsqrt(var + RMS_EPS) * g


def _dot(a, b):
    return jnp.dot(a, b, preferred_element_type=F32)


def _dot_nt(a, b):
    return lax.dot_general(a, b, (((1,), (1,)), ((), ())), preferred_element_type=F32)


def _split3(x):
    hi = x.astype(BF16)
    r1 = x - hi.astype(F32)
    mid = r1.astype(BF16)
    lo = (r1 - mid.astype(F32)).astype(BF16)
    return hi, mid, lo


def _ffn_kernel(n_cast, x_ref, xs_ref, gpre_ref, gpost_ref, gnext_ref, wg_ref, wu_ref, wd_ref,
                *refs):
    n = n_cast
    w32_refs, (y_ref, hn_ref, ys_ref, hns_ref) = refs[:n], refs[n:n + 4]
    w16_refs = refs[n + 4:2 * n + 4]
    xn_ref, acc_ref, xns_ref, accs_ref = refs[2 * n + 4:]
    f = pl.program_id(1)

    def rows(x_ref, y_ref, hn_ref, xn_ref, acc_ref, side_casts=False):
        last = pl.num_programs(1) - 1
        n_rows = x_ref.shape[0]
        chunk = min(n_rows, FFN_EDGE_CHUNK)
        chunks = [slice(r, r + chunk) for r in range(0, n_rows, chunk)]

        def casts():
            if side_casts:
                for w32_ref, w16_ref in zip(w32_refs, w16_refs):
                    w16_ref[...] = w32_ref[...].astype(BF16)

        def mlp(xn):
            g = _dot(xn, wg_ref[...])
            u = _dot(xn, wu_ref[...])
            a = (g * (1.0 / (1.0 + jnp.exp(-g))) * u).astype(BF16)
            return _dot(a, wd_ref[...])

        @pl.when(f == 0)
        def _():
            casts()
            for c in chunks:
                xn = _rms(x_ref[c, :], gpre_ref[...]).astype(BF16)
                xn_ref[c, :] = xn
                acc_ref[c, :] = mlp(xn)

        @pl.when(jnp.logical_and(f > 0, f < last))
        def _():
            casts()
            acc_ref[...] += mlp(xn_ref[...])

        @pl.when(f == last)
        def _():
            casts()
            for c in chunks:
                acc = acc_ref[c, :] + mlp(xn_ref[c, :])
                y = x_ref[c, :] + 0.5 * _rms(acc, gpost_ref[...])
                y_ref[c, :] = y
                hn_ref[c, :] = _rms(y, gnext_ref[...]).astype(BF16)

    rows(x_ref, y_ref, hn_ref, xn_ref, acc_ref, side_casts=True)

    @pl.when(pl.program_id(0) == pl.num_programs(0) - 1)
    def _():
        rows(xs_ref, ys_ref, hns_ref, xns_ref, accs_ref)


def _cast_tiling(w, n_row_blocks, n_steps):
    rows, cols = w.shape
    steps = max(s for s in range(1, n_steps + 1)
                if cols % s == 0 and (cols // s) % V7X_LANES == 0)
    return pl.BlockSpec((rows // n_row_blocks, cols // steps),
                        lambda i, f: (i, jnp.minimum(f, steps - 1)))


def _ffn(x, xs, g_pre, g_post, g_next, wg, wu, wd, *, tm, tf, cast=()):
    m, d = x.shape
    ms = xs.shape[0]
    d_ff = wg.shape[1]
    grid = (m // tm, d_ff // tf)
    row = pl.BlockSpec((tm, d), lambda i, f: (i, 0))
    small = pl.BlockSpec((ms, d), lambda i, f: (0, 0))
    vec = pl.BlockSpec((1, d), lambda i, f: (0, 0))
    cast_specs = [_cast_tiling(w, *grid) for w in cast]
    return pl.pallas_call(
        functools.partial(_ffn_kernel, len(cast)),
        grid=grid,
        in_specs=[row, small, vec, vec, vec,
                  pl.BlockSpec((d, tf), lambda i, f: (0, f)),
                  pl.BlockSpec((d, tf), lambda i, f: (0, f)),
                  pl.BlockSpec((tf, d), lambda i, f: (f, 0)), *cast_specs],
        out_specs=[row, row, small, small, *cast_specs],
        out_shape=[jax.ShapeDtypeStruct((m, d), F32), jax.ShapeDtypeStruct((m, d), BF16),
                   jax.ShapeDtypeStruct((ms, d), F32), jax.ShapeDtypeStruct((ms, d), BF16),
                   *[jax.ShapeDtypeStruct(w.shape, BF16) for w in cast]],
        scratch_shapes=[pltpu.VMEM((tm, d), BF16), pltpu.VMEM((tm, d), F32),
                        pltpu.VMEM((ms, d), BF16), pltpu.VMEM((ms, d), F32)],
        compiler_params=_params("arbitrary", "arbitrary"),
        name="ffn",
    )(x, xs, g_pre, g_post, g_next, wg, wu, wd, *cast)


def _qkv_kernel(h_ref, wq_ref, wk_ref, wv_ref, q_ref, k_ref, v_ref, *extra):
    h = h_ref[...]
    q_ref[...] = _dot(h, wq_ref[...])
    k = _dot(h, wk_ref[...])
    k_ref[...] = k
    v_ref[...] = _dot(h, wv_ref[...])
    if extra:
        kb_ref, km_ref = extra
        kb_ref[...] = k.astype(BF16)
        blocks, width = km_ref.shape
        km_ref[...] = jnp.sum(k.reshape(blocks, MOBA_BLOCK, width), axis=1) * (1.0 / MOBA_BLOCK)


def _qkv_proj(hb, w_in, *, tm, with_block_means):
    m, d = hb.shape
    tn = ATTN_WIDTH
    wspec = lambda part: pl.BlockSpec((d, tn), lambda i: (0, part))
    ospec = pl.BlockSpec((tm, tn), lambda i: (i, 0))
    f32o = jax.ShapeDtypeStruct((m, tn), F32)
    out_specs, out_shape = [ospec] * 3, [f32o] * 3
    if with_block_means:
        bpt = tm // MOBA_BLOCK
        out_specs = out_specs + [ospec, pl.BlockSpec((None, bpt, tn), lambda i: (i, 0, 0))]
        out_shape = out_shape + [jax.ShapeDtypeStruct((m, tn), BF16),
                                 jax.ShapeDtypeStruct((m // tm, bpt, tn), F32)]
    return pl.pallas_call(
        _qkv_kernel,
        grid=(m // tm,),
        in_specs=[pl.BlockSpec((tm, d), lambda i: (i, 0)), wspec(0), wspec(1), wspec(2)],
        out_specs=out_specs,
        out_shape=out_shape,
        compiler_params=_params("arbitrary"),
        name="qkv_proj",
    )(hb, w_in, w_in, w_in)


def _conv_prompt_kernel(h_ref, wb_ref, wc_ref, wu_ref, wconv_ref, conv_ref, tail_ref, zbuf_ref):
    i = pl.program_id(1)
    tm = h_ref.shape[0]
    halo = V7X_SUBLANES

    @pl.when(i == 0)
    def _():
        zbuf_ref[0:halo, :] = jnp.zeros((halo, zbuf_ref.shape[1]), F32)

    h = h_ref[...]
    z = _dot(h, wc_ref[...]) * _dot(h, wu_ref[...])
    zbuf_ref[halo:halo + tm, :] = z
    w = wconv_ref[...]
    y = zbuf_ref[halo:halo + tm, :] * w[CONV_K - 1:CONV_K, :]
    for j in range(CONV_K - 1):
        off = halo - (CONV_K - 1) + j
        y = y + zbuf_ref[off:off + tm, :] * w[j:j + 1, :]
    conv_ref[...] = (_dot(h, wb_ref[...]) * y).astype(BF16)
    last = zbuf_ref[tm:tm + halo, :]
    zbuf_ref[0:halo, :] = last
    tail_ref[...] = last


def _conv_prompt(hb, w_in, w_conv, *, tm, tc):
    m, d = hb.shape
    nj = CONV_WIDTH // tc
    base = 3 * ATTN_WIDTH // tc
    wspec = lambda part: pl.BlockSpec((d, tc), lambda j, i: (0, base + part * nj + j))
    return pl.pallas_call(
        _conv_prompt_kernel,
        grid=(nj, m // tm),
        in_specs=[pl.BlockSpec((tm, d), lambda j, i: (i, 0)), wspec(0), wspec(1), wspec(2),
                  pl.BlockSpec((CONV_K, tc), lambda j, i: (0, j))],
        out_specs=[pl.BlockSpec((tm, tc), lambda j, i: (i, j)),
                   pl.BlockSpec((V7X_SUBLANES, tc), lambda j, i: (0, j))],
        out_shape=[jax.ShapeDtypeStruct((m, CONV_WIDTH), BF16),
                   jax.ShapeDtypeStruct((V7X_SUBLANES, CONV_WIDTH), F32)],
        scratch_shapes=[pltpu.VMEM((tm + V7X_SUBLANES, tc), F32)],
        compiler_params=_params("arbitrary", "arbitrary"),
        name="conv_prompt",
    )(hb, w_in, w_in, w_in, w_conv)


def _conv_sample_kernel(h_ref, wb_ref, wc_ref, wu_ref, wconv_ref, st_ref, conv_ref, nst_ref):
    h = h_ref[...]
    z = _dot(h, wc_ref[...]) * _dot(h, wu_ref[...])
    w = wconv_ref[...]
    y = z * w[CONV_K - 1:CONV_K, :]
    for j in range(CONV_K - 1):
        y = y + st_ref[j] * w[j:j + 1, :]
    conv_ref[...] = (_dot(h, wb_ref[...]) * y).astype(BF16)
    for j in range(CONV_K - 2):
        nst_ref[j] = st_ref[j + 1]
    nst_ref[CONV_K - 2] = z


def _conv_sample(hb, w_in, w_conv, state_t, *, tc):
    m, d = hb.shape
    nj = CONV_WIDTH // tc
    base = 3 * ATTN_WIDTH // tc
    wspec = lambda part: pl.BlockSpec((d, tc), lambda j: (0, base + part * nj + j))
    sspec = pl.BlockSpec((CONV_K - 1, m, tc), lambda j: (0, 0, j))
    return pl.pallas_call(
        _conv_sample_kernel,
        grid=(nj,),
        in_specs=[pl.BlockSpec((m, d), lambda j: (0, 0)), wspec(0), wspec(1), wspec(2),
                  pl.BlockSpec((CONV_K, tc), lambda j: (0, j)), sspec],
        out_specs=[pl.BlockSpec((m, tc), lambda j: (0, j)), sspec],
        out_shape=[jax.ShapeDtypeStruct((m, CONV_WIDTH), BF16),
                   jax.ShapeDtypeStruct((CONV_K - 1, m, CONV_WIDTH), F32)],
        compiler_params=_params("arbitrary"),
        name="conv_sample",
    )(hb, w_in, w_in, w_in, w_conv, state_t)


def _top_blocks(gate, n_past, axis_len):
    blk = lax.broadcasted_iota(jnp.int32, gate.shape, 0)
    past = blk < n_past
    g = jnp.where(past, gate, -jnp.inf)
    keep = jnp.zeros(gate.shape, F32)
    chosen = []
    for _ in range(MOBA_TOPK):
        mx = jnp.max(g, axis=0, keepdims=True)
        idx = jnp.min(jnp.where(g == mx, blk, axis_len), axis=0, keepdims=True)
        pick = blk == idx
        keep = jnp.where(pick, jnp.where(past, 1.0, keep), keep)
        g = jnp.where(pick, -jnp.inf, g)
        chosen.append(idx)
    return jnp.where(keep > 0.0, 0.0, NEG_BIG), chosen


def _top_blocks_bias(gate, n_past, axis_len):
    return _top_blocks(gate, n_past, axis_len)[0]


def _attn_prompt_kernel(q_ref, kb_ref, v_ref, km_ref, o_ref, kx_ref, vt_ref, vtb_ref, s_ref, sd_ref):
    i = pl.program_id(1)
    n_heads, n_grp = vt_ref.shape[0], vt_ref.shape[1]
    blk = MOBA_BLOCK
    grp = ATTN_GROUP
    gk = grp * blk
    nb = n_grp * grp
    dh = HEAD_DIM

    @pl.when(i == 0)
    def _():
        col = lax.broadcasted_iota(jnp.int32, (blk, dh), 1)
        for h in range(n_heads):
            for c in range(nb):
                rows = slice(c * blk, (c + 1) * blk)
                lane0 = (c % grp) * blk
                vt = v_ref[rows, h * dh:(h + 1) * dh].T.astype(BF16)
                vt_ref[h, c // grp, :, lane0:lane0 + blk] = vt
                vtb_ref[h, c] = vt
                kx_ref[h, rows, 0:dh] = kb_ref[rows, h * dh:(h + 1) * dh]
                kx_ref[h, rows, dh:2 * dh] = jnp.where(col == c, 1.0, 0.0).astype(BF16)

    def scores(h, g, q_ext):
        kg = kx_ref[h, pl.ds(pl.multiple_of(g * gk, gk), gk), :]
        return _dot_nt(kg, q_ext)

    def update(h, g, slot, state):
        m, l, acc = state
        s = s_ref[h, slot]
        m_new = jnp.maximum(m, jnp.max(s, axis=0, keepdims=True))
        p = jnp.exp2(s - m_new)
        alpha = jnp.exp2(m - m_new)
        l = alpha * l + jnp.sum(p, axis=0, keepdims=True)
        acc = acc * alpha + _dot(vt_ref[h, g], p.astype(BF16))
        return m_new, l, acc

    qbs, gates, s0s, q_exts, state = [], [], [], [], []
    for h in range(n_heads):
        q = q_ref[:, h * dh:(h + 1) * dh]
        qbs.append((q * EXP2_SCALE).astype(BF16))
        q_hi, q_mid, _ = _split3(q)
        k_hi, k_mid, _ = _split3(km_ref[:, h * dh:(h + 1) * dh])
        gates.append(_dot_nt(k_hi, q_hi) + (_dot_nt(k_hi, q_mid) + _dot_nt(k_mid, q_hi)))

    for h in range(n_heads):
        s0s.append(_dot_nt(kb_ref[0:gk, h * dh:(h + 1) * dh], qbs[h]))
        k_own = kb_ref[pl.ds(pl.multiple_of(i * blk, blk), blk), h * dh:(h + 1) * dh]
        s = _dot_nt(k_own, qbs[h])
        key_pos = lax.broadcasted_iota(jnp.int32, s.shape, 0)
        qry_pos = lax.broadcasted_iota(jnp.int32, s.shape, 1)
        sd_ref[h] = jnp.where(key_pos <= qry_pos, s, NEG_BIG)

    for h in range(n_heads):
        bias = _top_blocks_bias(gates[h], i, nb)
        s_ref[h, 0] = (s0s[h].reshape(grp, blk, blk) + bias[0:grp][:, None, :]).reshape(gk, blk)
        bias = jnp.concatenate([bias, jnp.zeros((dh - nb, blk), F32)], axis=0)
        q_exts.append(jnp.concatenate([qbs[h], bias.T.astype(BF16)], axis=1))
        state.append((jnp.full((1, blk), NEG_BIG, F32), jnp.zeros((1, blk), F32),
                      jnp.zeros((dh, blk), F32)))

    n_groups = lax.shift_right_logical(i + (grp - 1), ATTN_GROUP_LOG2)
    n_pairs = lax.shift_right_logical(n_groups, 1)

    def body(u, carry):
        g0 = 2 * u
        for h in range(n_heads):
            s_ref[h, 1] = scores(h, g0 + 1, q_exts[h])
        carry = tuple(update(h, g0, 0, carry[h]) for h in range(n_heads))
        for h in range(n_heads):
            s_ref[h, 0] = scores(h, jnp.minimum(g0 + 2, n_grp - 1), q_exts[h])
        return tuple(update(h, g0 + 1, 1, carry[h]) for h in range(n_heads))

    state = lax.fori_loop(0, n_pairs, body, tuple(state))

    def finish(with_group):
        for h in range(n_heads):
            m, l, acc = state[h]
            sd = sd_ref[h]
            m_new = jnp.maximum(m, jnp.max(sd, axis=0, keepdims=True))
            if with_group:
                s = s_ref[h, 0]
                m_new = jnp.maximum(m_new, jnp.max(s, axis=0, keepdims=True))
            pd = jnp.exp2(sd - m_new)
            alpha = jnp.exp2(m - m_new)
            l = alpha * l + jnp.sum(pd, axis=0, keepdims=True)
            acc = acc * alpha + _dot(vtb_ref[h, i], pd.astype(BF16))
            if with_group:
                p = jnp.exp2(s - m_new)
                l = l + jnp.sum(p, axis=0, keepdims=True)
                acc = acc + _dot(vt_ref[h, 2 * n_pairs], p.astype(BF16))
            o_ref[:, h * dh:(h + 1) * dh] = (acc / l).T.astype(BF16)

    odd = 2 * n_pairs < n_groups
    pl.when(odd)(lambda: finish(True))
    pl.when(jnp.logical_not(odd))(lambda: finish(False))


def _attn_prompt(q, kb, v, km):
    t = q.shape[0]
    nb = t // MOBA_BLOCK
    hp = ATTN_HEADS_PER_STEP
    assert nb % ATTN_GROUP == 0 and ATTN_HEADS % hp == 0
    w = hp * HEAD_DIM
    return pl.pallas_call(
        _attn_prompt_kernel,
        grid=(ATTN_HEADS // hp, nb),
        in_specs=[pl.BlockSpec((MOBA_BLOCK, w), lambda h, i: (i, h)),
                  pl.BlockSpec((t, w), lambda h, i: (0, h)),
                  pl.BlockSpec((t, w), lambda h, i: (0, h)),
                  pl.BlockSpec((nb, w), lambda h, i: (0, h))],
        out_specs=pl.BlockSpec((MOBA_BLOCK, w), lambda h, i: (i, h)),
        out_shape=jax.ShapeDtypeStruct((t, ATTN_WIDTH), BF16),
        scratch_shapes=[pltpu.VMEM((hp, t, 2 * HEAD_DIM), BF16),
                        pltpu.VMEM((hp, nb // ATTN_GROUP, HEAD_DIM, ATTN_GROUP * MOBA_BLOCK), BF16),
                        pltpu.VMEM((hp, nb, HEAD_DIM, MOBA_BLOCK), BF16),
                        pltpu.VMEM((hp, 2, ATTN_GROUP * MOBA_BLOCK, MOBA_BLOCK), F32),
                        pltpu.VMEM((hp, MOBA_BLOCK, MOBA_BLOCK), F32)],
        compiler_params=_params("arbitrary", "arbitrary"),
        name="attn_prompt",
    )(q, kb, v, km)


def _attn_sample_kernel(n_pages, pt_ref, q_ref, kn_ref, vn_ref, *refs):
    k_refs = refs[:n_pages]
    cv_ref, o_ref, s_ref, vbuf_ref, stat_ref, pick_ref, sem = refs[n_pages:]
    t = pl.program_id(0)
    n_seq = pl.num_programs(0) - 1
    page = k_refs[0].shape[0]
    pages_per_blk = MOBA_BLOCK // page
    n_blk = n_pages // pages_per_blk
    n_heads, sub = ATTN_HEADS, V7X_SUBLANES
    head_picks = [(h, r) for h in range(n_heads) for r in range(MOBA_TOPK)]

    ch = SAMPLE_CHUNK
    chunks = [(pg, c * ch) for pg in range(n_pages) for c in range(page // ch)]

    def slab_copy(slot, seq, h, r, blk_idx, part):
        page_id = pt_ref[seq, blk_idx * pages_per_blk + part]
        return pltpu.make_async_copy(cv_ref.at[0, page_id, :, h, :],
                                     vbuf_ref.at[slot, h, r, pl.ds(part * page, page), :],
                                     sem.at[slot])

    @pl.when(t < n_seq)
    def _():
        slot = jnp.bitwise_and(t, 1)
        q = q_ref[...]
        qe = q * EXP2_SCALE

        def score(k):
            return jnp.broadcast_to(jnp.sum(k * qe, axis=-1, keepdims=True), k.shape)

        ksum = [None] * n_blk
        smax = [None] * n_blk
        for pg, r0 in chunks:
            b = pg // pages_per_blk
            k = k_refs[pg][r0:r0 + ch]
            s = score(k)
            row0 = (pg * page + r0) * sub
            s_ref[slot, row0:row0 + ch * sub, :] = s.reshape(ch * sub, HEAD_DIM)
            ks, cm = jnp.sum(k, axis=0), jnp.max(s, axis=0)
            ksum[b] = ks if ksum[b] is None else ksum[b] + ks
            smax[b] = cm if smax[b] is None else jnp.maximum(smax[b], cm)

        kmean = jnp.concatenate([ks[None] for ks in ksum], axis=0) * (1.0 / MOBA_BLOCK)
        gate = jnp.sum(kmean * q[None], axis=-1, keepdims=True)
        bias, chosen = _top_blocks(gate, n_blk, n_blk)
        for n, (h, r) in enumerate(head_picks):
            blk_idx = jnp.minimum(chosen[r][0, h, 0], n_blk - 1)
            pick_ref[slot, n] = blk_idx
            for part in range(pages_per_blk):
                slab_copy(slot, t, h, r, blk_idx, part).start()

        s_new = score(kn_ref[...])
        m = s_new
        for b in range(n_blk):
            m = jnp.maximum(m, smax[b] + bias[b])
        stat_ref[slot, 0] = m
        stat_ref[slot, 1] = s_new

    @pl.when(t > 0)
    def _():
        slot = jnp.bitwise_and(t + 1, 1)
        picks = [(h, r, pick_ref[slot, n]) for n, (h, r) in enumerate(head_picks)]
        for h, r, blk_idx in picks:
            for part in range(pages_per_blk):
                slab_copy(slot, t - 1, h, r, blk_idx, part).wait()

        m, s_new = stat_ref[slot, 0], stat_ref[slot, 1]
        fold = lambda x: jnp.sum(x.reshape(MOBA_BLOCK // sub, sub, HEAD_DIM), axis=0)
        for h in range(n_heads):
            m_h = m[h:h + 1, :]
            p_new = jnp.exp2(s_new[h:h + 1, :] - m_h)
            l_h = jnp.zeros((sub, HEAD_DIM), F32)
            acc_h = jnp.zeros((sub, HEAD_DIM), F32)
            for hh, r, blk_idx in picks:
                if hh != h:
                    continue
                rows = pl.ds(blk_idx * (MOBA_BLOCK * sub) + h, MOBA_BLOCK, stride=sub)
                p = jnp.exp2(s_ref[slot, rows, :] - m_h)
                l_h = l_h + fold(p)
                acc_h = acc_h + fold(p * vbuf_ref[slot, h, r])
            l_row = jnp.sum(l_h, axis=0, keepdims=True) + p_new
            acc_row = jnp.sum(acc_h, axis=0, keepdims=True) + p_new * vn_ref[h:h + 1, :]
            o_ref[h:h + 1, :] = acc_row / l_row


def _attn_sample(q, k_new, v_new, cache_k, cache_v, page_table):
    n_seq, n_pages = page_table.shape
    page = cache_k.shape[2]
    assert MOBA_BLOCK % page == 0 and (n_pages * page) % MOBA_BLOCK == 0
    assert n_pages * page // MOBA_BLOCK >= MOBA_TOPK and ATTN_HEADS == V7X_SUBLANES
    lanes = (ATTN_HEADS, HEAD_DIM)
    last = n_seq - 1
    key_row = pl.BlockSpec((None,) + lanes, lambda t, pt: (jnp.minimum(t, last), 0, 0))
    val_row = pl.BlockSpec((None,) + lanes, lambda t, pt: (jnp.maximum(t - 1, 0), 0, 0))
    page_spec = lambda pg: pl.BlockSpec((None, None, page) + lanes,
                                        lambda t, pt: (0, pt[jnp.minimum(t, last), pg], 0, 0, 0))
    return pl.pallas_call(
        functools.partial(_attn_sample_kernel, n_pages),
        grid_spec=pltpu.PrefetchScalarGridSpec(
            num_scalar_prefetch=1,
            grid=(n_seq + 1,),
            in_specs=([key_row, key_row, val_row] + [page_spec(pg) for pg in range(n_pages)]
                      + [pl.BlockSpec(memory_space=pl.ANY)]),
            out_specs=val_row,
            scratch_shapes=[pltpu.VMEM((2, n_pages * page * ATTN_HEADS, HEAD_DIM), F32),
                            pltpu.VMEM((2, ATTN_HEADS, MOBA_TOPK, MOBA_BLOCK, HEAD_DIM), F32),
                            pltpu.VMEM((2, 2) + lanes, F32),
                            pltpu.SMEM((2, ATTN_HEADS * MOBA_TOPK), jnp.int32),
                            pltpu.SemaphoreType.DMA((2,))],
        ),
        out_shape=jax.ShapeDtypeStruct((n_seq,) + lanes, F32),
        compiler_params=_params("arbitrary"),
        name="attn_sample",
    )(page_table, q, k_new, v_new, *([cache_k] * n_pages), cache_v)


def _out_proj_kernel(x_ref, a_ref, c_ref, xs_ref, as_ref, cs_ref, w_ref, g_ref, y_ref, ys_ref):
    def rows(x_ref, a_ref, c_ref, y_ref):
        chunk = min(x_ref.shape[0], FFN_EDGE_CHUNK)
        for r in range(0, x_ref.shape[0], chunk):
            c = slice(r, r + chunk)
            mix = (_dot(a_ref[c, :], w_ref[0:ATTN_WIDTH, :])
                   + _dot(c_ref[c, :], w_ref[ATTN_WIDTH:, :]))
            y_ref[c, :] = x_ref[c, :] + _rms(mix, g_ref[...])

    rows(x_ref, a_ref, c_ref, y_ref)

    @pl.when(pl.program_id(0) == pl.num_programs(0) - 1)
    def _():
        rows(xs_ref, as_ref, cs_ref, ys_ref)


def _out_proj(x, attn, conv, xs, attn_s, conv_s, w_out, g_post, *, tm):
    m, d = x.shape
    ms = xs.shape[0]
    row = lambda width: pl.BlockSpec((tm, width), lambda i: (i, 0))
    small = lambda width: pl.BlockSpec((ms, width), lambda i: (0, 0))
    return pl.pallas_call(
        _out_proj_kernel,
        grid=(m // tm,),
        in_specs=[row(d), row(ATTN_WIDTH), row(CONV_WIDTH),
                  small(d), small(ATTN_WIDTH), small(CONV_WIDTH),
                  pl.BlockSpec((d, d), lambda i: (0, 0)), pl.BlockSpec((1, d), lambda i: (0, 0))],
        out_specs=[row(d), small(d)],
        out_shape=[jax.ShapeDtypeStruct((m, d), F32), jax.ShapeDtypeStruct((ms, d), F32)],
        compiler_params=_params("arbitrary"),
        name="out_proj",
    )(x, attn, conv, xs, attn_s, conv_s, w_out, g_post)


def _ple_kernel(x_ref, hn_ref, p_ref, xs_ref, hns_ref, ps_ref, wg_ref, wp_ref, y_ref, ys_ref):
    def rows(x_ref, hn_ref, p_ref, y_ref):
        gate = _dot(hn_ref[...], wg_ref[...])
        emb = _dot(p_ref[...].astype(BF16), wp_ref[...])
        y_ref[...] = x_ref[...] + (1.0 / (1.0 + jnp.exp(-gate))) * emb

    rows(x_ref, hn_ref, p_ref, y_ref)

    @pl.when(pl.program_id(0) == pl.num_programs(0) - 1)
    def _():
        rows(xs_ref, hns_ref, ps_ref, ys_ref)


def _ple(x, hn, p, xs, hns, ps, w_gate, w_proj, *, tm):
    m, d = x.shape
    ms, pd = ps.shape
    row = lambda width: pl.BlockSpec((tm, width), lambda i: (i, 0))
    small = lambda width: pl.BlockSpec((ms, width), lambda i: (0, 0))
    return pl.pallas_call(
        _ple_kernel,
        grid=(m // tm,),
        in_specs=[row(d), row(d), row(pd), small(d), small(d), small(pd),
                  pl.BlockSpec((d, d), lambda i: (0, 0)),
                  pl.BlockSpec((pd, d), lambda i: (0, 0))],
        out_specs=[row(d), small(d)],
        out_shape=[jax.ShapeDtypeStruct((m, d), F32), jax.ShapeDtypeStruct((ms, d), F32)],
        compiler_params=_params("arbitrary"),
        name="ple",
    )(x, hn, p, xs, hns, ps, w_gate, w_proj)


def _tiles(m):
    return dict(tm=min(m, 512), tf=512)


def kernel(x_prompt, x_sample, cache_k, cache_v, state_conv, page_table, p_prompt, p_sample, w_in, w_out, w_conv, w_ffn1_gate, w_ffn1_up, w_ffn1_down, w_ffn2_gate, w_ffn2_up, w_ffn2_down, w_ple_gate, w_ple_proj, g_ffn1_pre, g_ffn1_post, g_mix_pre, g_mix_post, g_ffn2_pre, g_ffn2_post, g_ple):
    depth = w_in.shape[0]
    assert depth == 1 and x_prompt.shape[0] == 1 and x_sample.shape[1] == 1
    n_seq = x_sample.shape[0]
    seq = x_prompt.shape[1]

    bf = lambda w: w[0].astype(BF16)
    w1g, w1u, w1d = bf(w_ffn1_gate), bf(w_ffn1_up), bf(w_ffn1_down)
    wc = w_conv[0]

    tp, ts = _tiles(seq), _tiles(n_seq)
    heads = (ATTN_HEADS, HEAD_DIM)

    later = (w_ffn2_gate[0], w_ffn2_up[0], w_ffn2_down[0], w_in[0], w_out[0],
             w_ple_gate[0], w_ple_proj[0])
    xp1, hp, xs1, hs, w2g, w2u, w2d, w_in_b, w_out_b, w_pg, w_pp = _ffn(
        x_prompt[0], x_sample[:, 0, :], g_ffn1_pre, g_ffn1_post, g_mix_pre,
        w1g, w1u, w1d, tm=tp["tm"], tf=tp["tf"], cast=later)

    qp, kp, vp, kbp, kmp = _qkv_proj(hp, w_in_b, tm=tp["tm"], with_block_means=True)
    conv_p, tail_p = _conv_prompt(hp, w_in_b, wc, tm=512, tc=CONV_WIDTH)
    attn_p = _attn_prompt(qp, kbp, vp, kmp.reshape(seq // MOBA_BLOCK, ATTN_WIDTH))

    qs, ks, vs = _qkv_proj(hs, w_in_b, tm=ts["tm"], with_block_means=False)
    state_t = jnp.transpose(state_conv[0], (1, 0, 2))
    conv_s, nstate_t = _conv_sample(hs, w_in_b, wc, state_t, tc=512)
    qs3, ks3, vs3 = (a.reshape(n_seq, *heads) for a in (qs, ks, vs))
    attn_s = _attn_sample(qs3, ks3, vs3, cache_k, cache_v, page_table)
    attn_s = attn_s.reshape(n_seq, ATTN_WIDTH).astype(BF16)

    xp2, xs2 = _out_proj(xp1, attn_p, conv_p, xs1, attn_s, conv_s, w_out_b, g_mix_post,
                         tm=tp["tm"])
    xp3, hnp, xs3, hns = _ffn(xp2, xs2, g_ffn2_pre, g_ffn2_post, g_ple, w2g, w2u, w2d,
                              tm=tp["tm"], tf=tp["tf"])
    y_prompt, y_sample = _ple(xp3, hnp, p_prompt[0, 0], xs3, hns, p_sample[0, :, 0, :],
                              w_pg, w_pp, tm=tp["tm"])

    return (y_prompt[None],
            y_sample[:, None, :],
            kp.reshape(1, 1, seq, *heads),
            vp.reshape(1, 1, seq, *heads),
            tail_p[V7X_SUBLANES - (CONV_K - 1):][None, None],
            ks.reshape(1, n_seq, 1, *heads),
            vs.reshape(1, n_seq, 1, *heads),
            jnp.transpose(nstate_t, (1, 0, 2))[None])
```

```python
import functools
import math

import jax
import jax.numpy as jnp
from jax import lax
from jax.experimental import pallas as pl
from jax.experimental.pallas import tpu as pltpu

F32 = jnp.float32
BF16 = jnp.bfloat16

D_MODEL = 2048
HEAD_DIM = 128
ATTN_HEADS = 8
ATTN_WIDTH = ATTN_HEADS * HEAD_DIM
CONV_WIDTH = D_MODEL - ATTN_WIDTH
CONV_K = 3
MOBA_BLOCK = 256
MOBA_TOPK = 3
RMS_EPS = 1e-6

V7X_SUBLANES = 8
V7X_LANES = 128
V7X_VMEM_LIMIT_BYTES = 56 * 1024 * 1024

FFN_EDGE_CHUNK = 256
ATTN_GROUP_LOG2 = 2
ATTN_GROUP = 1 << ATTN_GROUP_LOG2
ATTN_HEADS_PER_STEP = 2
SAMPLE_CHUNK = 8

NEG_BIG = -1e30
EXP2_SCALE = (HEAD_DIM ** -0.5) * math.log2(math.e)


def _params(*semantics):
    return pltpu.CompilerParams(dimension_semantics=semantics,
                                vmem_limit_bytes=V7X_VMEM_LIMIT_BYTES)


def _rms(x, g):
    var = jnp.mean(x * x, axis=-1, keepdims=True)
    return x * lax.rsqrt(var + RMS_EPS) * g


def _dot(a, b):
    return jnp.dot(a, b, preferred_element_type=F32)


def _dot_nt(a, b):
    return lax.dot_general(a, b, (((1,), (1,)), ((), ())), preferred_element_type=F32)


def _split3(x):
    hi = x.astype(BF16)
    r1 = x - hi.astype(F32)
    mid = r1.astype(BF16)
    lo = (r1 - mid.astype(F32)).astype(BF16)
    return hi, mid, lo


def _ffn_kernel(n_cast, x_ref, xs_ref, gpre_ref, gpost_ref, gnext_ref, wg_ref, wu_ref, wd_ref,
                *refs):
    n = n_cast
    w32_refs, (y_ref, hn_ref, ys_ref, hns_ref) = refs[:n], refs[n:n + 4]
    w16_refs = refs[n + 4:2 * n + 4]
    xn_ref, acc_ref, xns_ref, accs_ref = refs[2 * n + 4:]
    f = pl.program_id(1)

    def rows(x_ref, y_ref, hn_ref, xn_ref, acc_ref, side_casts=False):
        last = pl.num_programs(1) - 1
        n_rows = x_ref.shape[0]
        chunk = min(n_rows, FFN_EDGE_CHUNK)
        chunks = [slice(r, r + chunk) for r in range(0, n_rows, chunk)]

        def casts():
            if side_casts:
                for w32_ref, w16_ref in zip(w32_refs, w16_refs):
                    w16_ref[...] = w32_ref[...].astype(BF16)

        def mlp(xn):
            g = _dot(xn, wg_ref[...])
            u = _dot(xn, wu_ref[...])
            a = (g * (1.0 / (1.0 + jnp.exp(-g))) * u).astype(BF16)
            return _dot(a, wd_ref[...])

        @pl.when(f == 0)
        def _():
            casts()
            for c in chunks:
                xn = _rms(x_ref[c, :], gpre_ref[...]).astype(BF16)
                xn_ref[c, :] = xn
                acc_ref[c, :] = mlp(xn)

        @pl.when(jnp.logical_and(f > 0, f < last))
        def _():
            casts()
            acc_ref[...] += mlp(xn_ref[...])

        @pl.when(f == last)
        def _():
            casts()
            for c in chunks:
                acc = acc_ref[c, :] + mlp(xn_ref[c, :])
                y = x_ref[c, :] + 0.5 * _rms(acc, gpost_ref[...])
                y_ref[c, :] = y
                hn_ref[c, :] = _rms(y, gnext_ref[...]).astype(BF16)

    rows(x_ref, y_ref, hn_ref, xn_ref, acc_ref, side_casts=True)

    @pl.when(pl.program_id(0) == pl.num_programs(0) - 1)
    def _():
        rows(xs_ref, ys_ref, hns_ref, xns_ref, accs_ref)


def _ffn_ring_kernel(x_ref, xs_ref, gpre_ref, gpost_ref, gnext_ref, wg_hbm, wu_hbm, wd_hbm,
                     y_ref, hn_ref, ys_ref, hns_ref, xn_ref, acc_ref, xns_ref, accs_ref,
                     wg_buf, wu_buf, wd_buf, sem):
    i = pl.program_id(0)
    n_i = pl.num_programs(0)
    n_slots, _, tf = wg_buf.shape
    n_f = wd_hbm.shape[0] // tf

    def copies(f, slot):
        cols = pl.ds(pl.multiple_of(f * tf, tf), tf)
        return (pltpu.make_async_copy(wg_hbm.at[:, cols], wg_buf.at[slot], sem.at[slot]),
                pltpu.make_async_copy(wu_hbm.at[:, cols], wu_buf.at[slot], sem.at[slot]),
                pltpu.make_async_copy(wd_hbm.at[cols, :], wd_buf.at[slot], sem.at[slot]))

    def start(f, slot):
        for c in copies(f, slot):
            c.start()

    def wait(f, slot):
        for c in copies(f, slot):
            c.wait()

    def mlp(xn, slot):
        g = _dot(xn, wg_buf[slot])
        u = _dot(xn, wu_buf[slot])
        a = (g * (1.0 / (1.0 + jnp.exp(-g))) * u).astype(BF16)
        return _dot(a, wd_buf[slot])

    def chunks_of(ref):
        chunk = min(ref.shape[0], FFN_EDGE_CHUNK)
        return [slice(r, r + chunk) for r in range(0, ref.shape[0], chunk)]

    def first_rows(slot, x_ref, y_ref, hn_ref, xn_ref, acc_ref):
        for c in chunks_of(x_ref):
            xn = _rms(x_ref[c, :], gpre_ref[...]).astype(BF16)
            xn_ref[c, :] = xn
            acc_ref[c, :] = mlp(xn, slot)

    def mid_rows(slot, x_ref, y_ref, hn_ref, xn_ref, acc_ref):
        acc_ref[...] += mlp(xn_ref[...], slot)

    def last_rows(slot, x_ref, y_ref, hn_ref, xn_ref, acc_ref):
        for c in chunks_of(x_ref):
            acc = acc_ref[c, :] + mlp(xn_ref[c, :], slot)
            y = x_ref[c, :] + 0.5 * _rms(acc, gpost_ref[...])
            y_ref[c, :] = y
            hn_ref[c, :] = _rms(y, gnext_ref[...]).astype(BF16)

    def both_groups(rows, slot):
        rows(slot, x_ref, y_ref, hn_ref, xn_ref, acc_ref)
        pl.when(i == n_i - 1)(lambda: rows(slot, xs_ref, ys_ref, hns_ref, xns_ref, accs_ref))

    pl.when(i == 0)(lambda: start(0, 0))
    wait(0, 0)
    start(1, 1)
    both_groups(first_rows, 0)

    def body(f, carry):
        slot = lax.rem(f, n_slots)
        wait(f, slot)
        start(f + 1, lax.rem(f + 1, n_slots))
        both_groups(mid_rows, slot)
        return carry

    lax.fori_loop(1, n_f - 1, body, 0)

    last_slot = (n_f - 1) % n_slots
    wait(n_f - 1, last_slot)
    pl.when(i < n_i - 1)(lambda: start(0, 0))
    both_groups(last_rows, last_slot)


def _ffn_ring(x, xs, g_pre, g_post, g_next, wg, wu, wd, *, tm, tf):
    m, d = x.shape
    ms = xs.shape[0]
    n_f = wg.shape[1] // tf
    n_slots = 3
    assert n_f >= 3 and (n_f - 1) % n_slots != 0
    row = pl.BlockSpec((tm, d), lambda i: (i, 0))
    small = pl.BlockSpec((ms, d), lambda i: (0, 0))
    vec = pl.BlockSpec((1, d), lambda i: (0, 0))
    hbm = pl.BlockSpec(memory_space=pl.ANY)
    return pl.pallas_call(
        _ffn_ring_kernel,
        grid=(m // tm,),
        in_specs=[row, small, vec, vec, vec, hbm, hbm, hbm],
        out_specs=[row, row, small, small],
        out_shape=[jax.ShapeDtypeStruct((m, d), F32), jax.ShapeDtypeStruct((m, d), BF16),
                   jax.ShapeDtypeStruct((ms, d), F32), jax.ShapeDtypeStruct((ms, d), BF16)],
        scratch_shapes=[pltpu.VMEM((tm, d), BF16), pltpu.VMEM((tm, d), F32),
                        pltpu.VMEM((ms, d), BF16), pltpu.VMEM((ms, d), F32),
                        pltpu.VMEM((n_slots, d, tf), BF16), pltpu.VMEM((n_slots, d, tf), BF16),
                        pltpu.VMEM((n_slots, tf, d), BF16), pltpu.SemaphoreType.DMA((n_slots,))],
        compiler_params=_params("arbitrary"),
        name="ffn_ring",
    )(x, xs, g_pre, g_post, g_next, wg, wu, wd)


def _cast_tiling(w, n_row_blocks, n_steps):
    rows, cols = w.shape
    steps = max(s for s in range(1, n_steps + 1)
                if cols % s == 0 and (cols // s) % V7X_LANES == 0)
    return pl.BlockSpec((rows // n_row_blocks, cols // steps),
                        lambda i, f: (i, jnp.minimum(f, steps - 1)))


def _ffn(x, xs, g_pre, g_post, g_next, wg, wu, wd, *, tm, tf, cast=()):
    m, d = x.shape
    ms = xs.shape[0]
    d_ff = wg.shape[1]
    grid = (m // tm, d_ff // tf)
    row = pl.BlockSpec((tm, d), lambda i, f: (i, 0))
    small = pl.BlockSpec((ms, d), lambda i, f: (0, 0))
    vec = pl.BlockSpec((1, d), lambda i, f: (0, 0))
    cast_specs = [_cast_tiling(w, *grid) for w in cast]
    return pl.pallas_call(
        functools.partial(_ffn_kernel, len(cast)),
        grid=grid,
        in_specs=[row, small, vec, vec, vec,
                  pl.BlockSpec((d, tf), lambda i, f: (0, f)),
                  pl.BlockSpec((d, tf), lambda i, f: (0, f)),
                  pl.BlockSpec((tf, d), lambda i, f: (f, 0)), *cast_specs],
        out_specs=[row, row, small, small, *cast_specs],
        out_shape=[jax.ShapeDtypeStruct((m, d), F32), jax.ShapeDtypeStruct((m, d), BF16),
                   jax.ShapeDtypeStruct((ms, d), F32), jax.ShapeDtypeStruct((ms, d), BF16),
                   *[jax.ShapeDtypeStruct(w.shape, BF16) for w in cast]],
        scratch_shapes=[pltpu.VMEM((tm, d), BF16), pltpu.VMEM((tm, d), F32),
                        pltpu.VMEM((ms, d), BF16), pltpu.VMEM((ms, d), F32)],
        compiler_params=_params("arbitrary", "arbitrary"),
        name="ffn",
    )(x, xs, g_pre, g_post, g_next, wg, wu, wd, *cast)


def _qkv_kernel(h_ref, wq_ref, wk_ref, wv_ref, q_ref, k_ref, v_ref, *extra):
    h = h_ref[...]
    q_ref[...] = _dot(h, wq_ref[...])
    k = _dot(h, wk_ref[...])
    k_ref[...] = k
    v_ref[...] = _dot(h, wv_ref[...])
    if extra:
        kb_ref, km_ref = extra
        kb_ref[...] = k.astype(BF16)
        blocks, width = km_ref.shape
        km_ref[...] = jnp.sum(k.reshape(blocks, MOBA_BLOCK, width), axis=1) * (1.0 / MOBA_BLOCK)


def _qkv_proj(hb, w_in, *, tm, with_block_means):
    m, d = hb.shape
    tn = ATTN_WIDTH
    wspec = lambda part: pl.BlockSpec((d, tn), lambda i: (0, part))
    ospec = pl.BlockSpec((tm, tn), lambda i: (i, 0))
    f32o = jax.ShapeDtypeStruct((m, tn), F32)
    out_specs, out_shape = [ospec] * 3, [f32o] * 3
    if with_block_means:
        bpt = tm // MOBA_BLOCK
        out_specs = out_specs + [ospec, pl.BlockSpec((None, bpt, tn), lambda i: (i, 0, 0))]
        out_shape = out_shape + [jax.ShapeDtypeStruct((m, tn), BF16),
                                 jax.ShapeDtypeStruct((m // tm, bpt, tn), F32)]
    return pl.pallas_call(
        _qkv_kernel,
        grid=(m // tm,),
        in_specs=[pl.BlockSpec((tm, d), lambda i: (i, 0)), wspec(0), wspec(1), wspec(2)],
        out_specs=out_specs,
        out_shape=out_shape,
        compiler_params=_params("arbitrary"),
        name="qkv_proj",
    )(hb, w_in, w_in, w_in)


def _conv_prompt_kernel(h_ref, wb_ref, wc_ref, wu_ref, wconv_ref, conv_ref, tail_ref, zbuf_ref):
    i = pl.program_id(1)
    tm = h_ref.shape[0]
    halo = V7X_SUBLANES

    @pl.when(i == 0)
    def _():
        zbuf_ref[0:halo, :] = jnp.zeros((halo, zbuf_ref.shape[1]), F32)

    h = h_ref[...]
    z = _dot(h, wc_ref[...]) * _dot(h, wu_ref[...])
    zbuf_ref[halo:halo + tm, :] = z
    w = wconv_ref[...]
    y = zbuf_ref[halo:halo + tm, :] * w[CONV_K - 1:CONV_K, :]
    for j in range(CONV_K - 1):
        off = halo - (CONV_K - 1) + j
        y = y + zbuf_ref[off:off + tm, :] * w[j:j + 1, :]
    conv_ref[...] = (_dot(h, wb_ref[...]) * y).astype(BF16)
    last = zbuf_ref[tm:tm + halo, :]
    zbuf_ref[0:halo, :] = last
    tail_ref[...] = last


def _conv_prompt(hb, w_in, w_conv, *, tm, tc):
    m, d = hb.shape
    nj = CONV_WIDTH // tc
    base = 3 * ATTN_WIDTH // tc
    wspec = lambda part: pl.BlockSpec((d, tc), lambda j, i: (0, base + part * nj + j))
    return pl.pallas_call(
        _conv_prompt_kernel,
        grid=(nj, m // tm),
        in_specs=[pl.BlockSpec((tm, d), lambda j, i: (i, 0)), wspec(0), wspec(1), wspec(2),
                  pl.BlockSpec((CONV_K, tc), lambda j, i: (0, j))],
        out_specs=[pl.BlockSpec((tm, tc), lambda j, i: (i, j)),
                   pl.BlockSpec((V7X_SUBLANES, tc), lambda j, i: (0, j))],
        out_shape=[jax.ShapeDtypeStruct((m, CONV_WIDTH), BF16),
                   jax.ShapeDtypeStruct((V7X_SUBLANES, CONV_WIDTH), F32)],
        scratch_shapes=[pltpu.VMEM((tm + V7X_SUBLANES, tc), F32)],
        compiler_params=_params("arbitrary", "arbitrary"),
        name="conv_prompt",
    )(hb, w_in, w_in, w_in, w_conv)


def _conv_sample_kernel(h_ref, wb_ref, wc_ref, wu_ref, wconv_ref, st_ref, conv_ref, nst_ref):
    h = h_ref[...]
    z = _dot(h, wc_ref[...]) * _dot(h, wu_ref[...])
    w = wconv_ref[...]
    y = z * w[CONV_K - 1:CONV_K, :]
    for j in range(CONV_K - 1):
        y = y + st_ref[j] * w[j:j + 1, :]
    conv_ref[...] = (_dot(h, wb_ref[...]) * y).astype(BF16)
    for j in range(CONV_K - 2):
        nst_ref[j] = st_ref[j + 1]
    nst_ref[CONV_K - 2] = z


def _conv_sample(hb, w_in, w_conv, state_t, *, tc):
    m, d = hb.shape
    nj = CONV_WIDTH // tc
    base = 3 * ATTN_WIDTH // tc
    wspec = lambda part: pl.BlockSpec((d, tc), lambda j: (0, base + part * nj + j))
    sspec = pl.BlockSpec((CONV_K - 1, m, tc), lambda j: (0, 0, j))
    return pl.pallas_call(
        _conv_sample_kernel,
        grid=(nj,),
        in_specs=[pl.BlockSpec((m, d), lambda j: (0, 0)), wspec(0), wspec(1), wspec(2),
                  pl.BlockSpec((CONV_K, tc), lambda j: (0, j)), sspec],
        out_specs=[pl.BlockSpec((m, tc), lambda j: (0, j)), sspec],
        out_shape=[jax.ShapeDtypeStruct((m, CONV_WIDTH), BF16),
                   jax.ShapeDtypeStruct((CONV_K - 1, m, CONV_WIDTH), F32)],
        compiler_params=_params("arbitrary"),
        name="conv_sample",
    )(hb, w_in, w_in, w_in, w_conv, state_t)


def _top_blocks(gate, n_past, axis_len):
    blk = lax.broadcasted_iota(jnp.int32, gate.shape, 0)
    past = blk < n_past
    g = jnp.where(past, gate, -jnp.inf)
    keep = jnp.zeros(gate.shape, F32)
    chosen = []
    for _ in range(MOBA_TOPK):
        mx = jnp.max(g, axis=0, keepdims=True)
        idx = jnp.min(jnp.where(g == mx, blk, axis_len), axis=0, keepdims=True)
        pick = blk == idx
        keep = jnp.where(pick, jnp.where(past, 1.0, keep), keep)
        g = jnp.where(pick, -jnp.inf, g)
        chosen.append(idx)
    return jnp.where(keep > 0.0, 0.0, NEG_BIG), chosen


def _top_blocks_bias(gate, n_past, axis_len):
    return _top_blocks(gate, n_past, axis_len)[0]


def _attn_prompt_kernel(q_ref, kb_ref, v_ref, km_ref, o_ref, kx_ref, vt_ref, vtb_ref, s_ref, sd_ref):
    i = pl.program_id(1)
    n_heads, n_grp = vt_ref.shape[0], vt_ref.shape[1]
    blk = MOBA_BLOCK
    grp = ATTN_GROUP
    gk = grp * blk
    nb = n_grp * grp
    dh = HEAD_DIM

    @pl.when(i == 0)
    def _():
        col = lax.broadcasted_iota(jnp.int32, (blk, dh), 1)
        for h in range(n_heads):
            for c in range(nb):
                rows = slice(c * blk, (c + 1) * blk)
                lane0 = (c % grp) * blk
                vt = v_ref[rows, h * dh:(h + 1) * dh].T.astype(BF16)
                vt_ref[h, c // grp, :, lane0:lane0 + blk] = vt
                vtb_ref[h, c] = vt
                kx_ref[h, rows, 0:dh] = kb_ref[rows, h * dh:(h + 1) * dh]
                kx_ref[h, rows, dh:2 * dh] = jnp.where(col == c, 1.0, 0.0).astype(BF16)

    def scores(h, g, q_ext):
        kg = kx_ref[h, pl.ds(pl.multiple_of(g * gk, gk), gk), :]
        return _dot_nt(kg, q_ext)

    def update(h, g, slot, state):
        m, l, acc = state
        s = s_ref[h, slot]
        m_new = jnp.maximum(m, jnp.max(s, axis=0, keepdims=True))
        p = jnp.exp2(s - m_new)
        alpha = jnp.exp2(m - m_new)
        l = alpha * l + jnp.sum(p, axis=0, keepdims=True)
        acc = acc * alpha + _dot(vt_ref[h, g], p.astype(BF16))
        return m_new, l, acc

    qbs, gates, s0s, q_exts, state = [], [], [], [], []
    for h in range(n_heads):
        q = q_ref[:, h * dh:(h + 1) * dh]
        qbs.append((q * EXP2_SCALE).astype(BF16))
        q_hi, q_mid, _ = _split3(q)
        k_hi, k_mid, _ = _split3(km_ref[:, h * dh:(h + 1) * dh])
        gates.append(_dot_nt(k_hi, q_hi) + (_dot_nt(k_hi, q_mid) + _dot_nt(k_mid, q_hi)))

    for h in range(n_heads):
        s0s.append(_dot_nt(kb_ref[0:gk, h * dh:(h + 1) * dh], qbs[h]))
        k_own = kb_ref[pl.ds(pl.multiple_of(i * blk, blk), blk), h * dh:(h + 1) * dh]
        s = _dot_nt(k_own, qbs[h])
        key_pos = lax.broadcasted_iota(jnp.int32, s.shape, 0)
        qry_pos = lax.broadcasted_iota(jnp.int32, s.shape, 1)
        sd_ref[h] = jnp.where(key_pos <= qry_pos, s, NEG_BIG)

    for h in range(n_heads):
        bias = _top_blocks_bias(gates[h], i, nb)
        s_ref[h, 0] = (s0s[h].reshape(grp, blk, blk) + bias[0:grp][:, None, :]).reshape(gk, blk)
        bias = jnp.concatenate([bias, jnp.zeros((dh - nb, blk), F32)], axis=0)
        q_exts.append(jnp.concatenate([qbs[h], bias.T.astype(BF16)], axis=1))
        state.append((jnp.full((1, blk), NEG_BIG, F32), jnp.zeros((1, blk), F32),
                      jnp.zeros((dh, blk), F32)))

    n_groups = lax.shift_right_logical(i + (grp - 1), ATTN_GROUP_LOG2)
    n_pairs = lax.shift_right_logical(n_groups, 1)

    def body(u, carry):
        g0 = 2 * u
        for h in range(n_heads):
            s_ref[h, 1] = scores(h, g0 + 1, q_exts[h])
        carry = tuple(update(h, g0, 0, carry[h]) for h in range(n_heads))
        for h in range(n_heads):
            s_ref[h, 0] = scores(h, jnp.minimum(g0 + 2, n_grp - 1), q_exts[h])
        return tuple(update(h, g0 + 1, 1, carry[h]) for h in range(n_heads))

    state = lax.fori_loop(0, n_pairs, body, tuple(state))

    def finish(with_group):
        for h in range(n_heads):
            m, l, acc = state[h]
            sd = sd_ref[h]
            m_new = jnp.maximum(m, jnp.max(sd, axis=0, keepdims=True))
            if with_group:
                s = s_ref[h, 0]
                m_new = jnp.maximum(m_new, jnp.max(s, axis=0, keepdims=True))
            pd = jnp.exp2(sd - m_new)
            alpha = jnp.exp2(m - m_new)
            l = alpha * l + jnp.sum(pd, axis=0, keepdims=True)
            acc = acc * alpha + _dot(vtb_ref[h, i], pd.astype(BF16))
            if with_group:
                p = jnp.exp2(s - m_new)
                l = l + jnp.sum(p, axis=0, keepdims=True)
                acc = acc + _dot(vt_ref[h, 2 * n_pairs], p.astype(BF16))
            o_ref[:, h * dh:(h + 1) * dh] = (acc / l).T.astype(BF16)

    odd = 2 * n_pairs < n_groups
    pl.when(odd)(lambda: finish(True))
    pl.when(jnp.logical_not(odd))(lambda: finish(False))


def _attn_prompt(q, kb, v, km):
    t = q.shape[0]
    nb = t // MOBA_BLOCK
    hp = ATTN_HEADS_PER_STEP
    assert nb % ATTN_GROUP == 0 and ATTN_HEADS % hp == 0
    w = hp * HEAD_DIM
    return pl.pallas_call(
        _attn_prompt_kernel,
        grid=(ATTN_HEADS // hp, nb),
        in_specs=[pl.BlockSpec((MOBA_BLOCK, w), lambda h, i: (i, h)),
                  pl.BlockSpec((t, w), lambda h, i: (0, h)),
                  pl.BlockSpec((t, w), lambda h, i: (0, h)),
                  pl.BlockSpec((nb, w), lambda h, i: (0, h))],
        out_specs=pl.BlockSpec((MOBA_BLOCK, w), lambda h, i: (i, h)),
        out_shape=jax.ShapeDtypeStruct((t, ATTN_WIDTH), BF16),
        scratch_shapes=[pltpu.VMEM((hp, t, 2 * HEAD_DIM), BF16),
                        pltpu.VMEM((hp, nb // ATTN_GROUP, HEAD_DIM, ATTN_GROUP * MOBA_BLOCK), BF16),
                        pltpu.VMEM((hp, nb, HEAD_DIM, MOBA_BLOCK), BF16),
                        pltpu.VMEM((hp, 2, ATTN_GROUP * MOBA_BLOCK, MOBA_BLOCK), F32),
                        pltpu.VMEM((hp, MOBA_BLOCK, MOBA_BLOCK), F32)],
        compiler_params=_params("arbitrary", "arbitrary"),
        name="attn_prompt",
    )(q, kb, v, km)


def _attn_sample_kernel(n_pages, pt_ref, q_ref, kn_ref, vn_ref, *refs):
    k_refs = refs[:n_pages]
    cv_ref, o_ref, s_ref, vbuf_ref, stat_ref, pick_ref, sem = refs[n_pages:]
    t = pl.program_id(0)
    n_seq = pl.num_programs(0) - 1
    page = k_refs[0].shape[0]
    pages_per_blk = MOBA_BLOCK // page
    n_blk = n_pages // pages_per_blk
    n_heads, sub = ATTN_HEADS, V7X_SUBLANES
    head_picks = [(h, r) for h in range(n_heads) for r in range(MOBA_TOPK)]

    ch = SAMPLE_CHUNK
    chunks = [(pg, c * ch) for pg in range(n_pages) for c in range(page // ch)]

    def slab_copy(slot, seq, h, r, blk_idx, part):
        page_id = pt_ref[seq, blk_idx * pages_per_blk + part]
        return pltpu.make_async_copy(cv_ref.at[0, page_id, :, h, :],
                                     vbuf_ref.at[slot, h, r, pl.ds(part * page, page), :],
                                     sem.at[slot])

    @pl.when(t < n_seq)
    def _():
        slot = jnp.bitwise_and(t, 1)
        q = q_ref[...]
        qe = q * EXP2_SCALE

        def score(k):
            return jnp.broadcast_to(jnp.sum(k * qe, axis=-1, keepdims=True), k.shape)

        ksum = [None] * n_blk
        smax = [None] * n_blk
        for pg, r0 in chunks:
            b = pg // pages_per_blk
            k = k_refs[pg][r0:r0 + ch]
            s = score(k)
            row0 = (pg * page + r0) * sub
            s_ref[slot, row0:row0 + ch * sub, :] = s.reshape(ch * sub, HEAD_DIM)
            ks, cm = jnp.sum(k, axis=0), jnp.max(s, axis=0)
            ksum[b] = ks if ksum[b] is None else ksum[b] + ks
            smax[b] = cm if smax[b] is None else jnp.maximum(smax[b], cm)

        kmean = jnp.concatenate([ks[None] for ks in ksum], axis=0) * (1.0 / MOBA_BLOCK)
        gate = jnp.sum(kmean * q[None], axis=-1, keepdims=True)
        bias, chosen = _top_blocks(gate, n_blk, n_blk)
        for n, (h, r) in enumerate(head_picks):
            blk_idx = jnp.minimum(chosen[r][0, h, 0], n_blk - 1)
            pick_ref[slot, n] = blk_idx
            for part in range(pages_per_blk):
                slab_copy(slot, t, h, r, blk_idx, part).start()

        s_new = score(kn_ref[...])
        m = s_new
        for b in range(n_blk):
            m = jnp.maximum(m, smax[b] + bias[b])
        stat_ref[slot, 0] = m
        stat_ref[slot, 1] = s_new

    @pl.when(t > 0)
    def _():
        slot = jnp.bitwise_and(t + 1, 1)
        picks = [(h, r, pick_ref[slot, n]) for n, (h, r) in enumerate(head_picks)]
        for h, r, blk_idx in picks:
            for part in range(pages_per_blk):
                slab_copy(slot, t - 1, h, r, blk_idx, part).wait()

        m, s_new = stat_ref[slot, 0], stat_ref[slot, 1]
        fold = lambda x: jnp.sum(x.reshape(MOBA_BLOCK // sub, sub, HEAD_DIM), axis=0)
        for h in range(n_heads):
            m_h = m[h:h + 1, :]
            p_new = jnp.exp2(s_new[h:h + 1, :] - m_h)
            l_h = jnp.zeros((sub, HEAD_DIM), F32)
            acc_h = jnp.zeros((sub, HEAD_DIM), F32)
            for hh, r, blk_idx in picks:
                if hh != h:
                    continue
                rows = pl.ds(blk_idx * (MOBA_BLOCK * sub) + h, MOBA_BLOCK, stride=sub)
                p = jnp.exp2(s_ref[slot, rows, :] - m_h)
                l_h = l_h + fold(p)
                acc_h = acc_h + fold(p * vbuf_ref[slot, h, r])
            l_row = jnp.sum(l_h, axis=0, keepdims=True) + p_new
            acc_row = jnp.sum(acc_h, axis=0, keepdims=True) + p_new * vn_ref[h:h + 1, :]
            o_ref[h:h + 1, :] = acc_row / l_row


def _attn_sample(q, k_new, v_new, cache_k, cache_v, page_table):
    n_seq, n_pages = page_table.shape
    page = cache_k.shape[2]
    assert MOBA_BLOCK % page == 0 and (n_pages * page) % MOBA_BLOCK == 0
    assert n_pages * page // MOBA_BLOCK >= MOBA_TOPK and ATTN_HEADS == V7X_SUBLANES
    lanes = (ATTN_HEADS, HEAD_DIM)
    last = n_seq - 1
    key_row = pl.BlockSpec((None,) + lanes, lambda t, pt: (jnp.minimum(t, last), 0, 0))
    val_row = pl.BlockSpec((None,) + lanes, lambda t, pt: (jnp.maximum(t - 1, 0), 0, 0))
    page_spec = lambda pg: pl.BlockSpec((None, None, page) + lanes,
                                        lambda t, pt: (0, pt[jnp.minimum(t, last), pg], 0, 0, 0))
    return pl.pallas_call(
        functools.partial(_attn_sample_kernel, n_pages),
        grid_spec=pltpu.PrefetchScalarGridSpec(
            num_scalar_prefetch=1,
            grid=(n_seq + 1,),
            in_specs=([key_row, key_row, val_row] + [page_spec(pg) for pg in range(n_pages)]
                      + [pl.BlockSpec(memory_space=pl.ANY)]),
            out_specs=val_row,
            scratch_shapes=[pltpu.VMEM((2, n_pages * page * ATTN_HEADS, HEAD_DIM), F32),
                            pltpu.VMEM((2, ATTN_HEADS, MOBA_TOPK, MOBA_BLOCK, HEAD_DIM), F32),
                            pltpu.VMEM((2, 2) + lanes, F32),
                            pltpu.SMEM((2, ATTN_HEADS * MOBA_TOPK), jnp.int32),
                            pltpu.SemaphoreType.DMA((2,))],
        ),
        out_shape=jax.ShapeDtypeStruct((n_seq,) + lanes, F32),
        compiler_params=_params("arbitrary"),
        name="attn_sample",
    )(page_table, q, k_new, v_new, *([cache_k] * n_pages), cache_v)


def _out_proj_kernel(x_ref, a_ref, c_ref, w_ref, g_ref, y_ref):
    mix = _dot(a_ref[...], w_ref[0:ATTN_WIDTH, :]) + _dot(c_ref[...], w_ref[ATTN_WIDTH:, :])
    y_ref[...] = x_ref[...] + _rms(mix, g_ref[...])


def _out_proj(x, attn, conv, w_out, g_post, *, tm):
    m, d = x.shape
    row = lambda width: pl.BlockSpec((tm, width), lambda i: (i, 0))
    return pl.pallas_call(
        _out_proj_kernel,
        grid=(m // tm,),
        in_specs=[row(d), row(ATTN_WIDTH), row(CONV_WIDTH),
                  pl.BlockSpec((d, d), lambda i: (0, 0)), pl.BlockSpec((1, d), lambda i: (0, 0))],
        out_specs=row(d),
        out_shape=jax.ShapeDtypeStruct((m, d), F32),
        compiler_params=_params("arbitrary"),
        name="out_proj",
    )(x, attn, conv, w_out, g_post)


def _ple_kernel(x_ref, hn_ref, p_ref, wg_ref, wp_ref, y_ref):
    gate = _dot(hn_ref[...], wg_ref[...])
    emb = _dot(p_ref[...].astype(BF16), wp_ref[...])
    y_ref[...] = x_ref[...] + (1.0 / (1.0 + jnp.exp(-gate))) * emb


def _ple(x, hn, p, w_gate, w_proj, *, tm, tn):
    m, d = x.shape
    pd = p.shape[1]
    return pl.pallas_call(
        _ple_kernel,
        grid=(m // tm, d // tn),
        in_specs=[pl.BlockSpec((tm, tn), lambda i, j: (i, j)),
                  pl.BlockSpec((tm, d), lambda i, j: (i, 0)),
                  pl.BlockSpec((tm, pd), lambda i, j: (i, 0)),
                  pl.BlockSpec((d, tn), lambda i, j: (0, j)),
                  pl.BlockSpec((pd, tn), lambda i, j: (0, j))],
        out_specs=pl.BlockSpec((tm, tn), lambda i, j: (i, j)),
        out_shape=jax.ShapeDtypeStruct((m, d), F32),
        compiler_params=_params("arbitrary", "arbitrary"),
        name="ple",
    )(x, hn, p, w_gate, w_proj)


def _tiles(m):
    tm = min(m, 512)
    return dict(tm=tm, tf=512, tn=ATTN_WIDTH, tp=D_MODEL)


def kernel(x_prompt, x_sample, cache_k, cache_v, state_conv, page_table, p_prompt, p_sample, w_in, w_out, w_conv, w_ffn1_gate, w_ffn1_up, w_ffn1_down, w_ffn2_gate, w_ffn2_up, w_ffn2_down, w_ple_gate, w_ple_proj, g_ffn1_pre, g_ffn1_post, g_mix_pre, g_mix_post, g_ffn2_pre, g_ffn2_post, g_ple):
    depth = w_in.shape[0]
    assert depth == 1 and x_prompt.shape[0] == 1 and x_sample.shape[1] == 1
    n_seq = x_sample.shape[0]
    seq = x_prompt.shape[1]
    n_phys, page = cache_k.shape[1], cache_k.shape[2]

    bf = lambda w: w[0].astype(BF16)
    w1g, w1u, w1d = bf(w_ffn1_gate), bf(w_ffn1_up), bf(w_ffn1_down)
    wc = w_conv[0]

    tp, ts = _tiles(seq), _tiles(n_seq)
    heads = (ATTN_HEADS, HEAD_DIM)

    later = (w_ffn2_gate[0], w_ffn2_up[0], w_ffn2_down[0], w_in[0], w_out[0],
             w_ple_gate[0], w_ple_proj[0])
    xp1, hp, xs1, hs, w2g, w2u, w2d, w_in_b, w_out_b, w_pg, w_pp = _ffn(
        x_prompt[0], x_sample[:, 0, :], g_ffn1_pre, g_ffn1_post, g_mix_pre,
        w1g, w1u, w1d, tm=tp["tm"], tf=tp["tf"], cast=later)

    qp, kp, vp, kbp, kmp = _qkv_proj(hp, w_in_b, tm=tp["tm"], with_block_means=True)
    conv_p, tail_p = _conv_prompt(hp, w_in_b, wc, tm=512, tc=512)
    attn_p = _attn_prompt(qp, kbp, vp, kmp.reshape(seq // MOBA_BLOCK, ATTN_WIDTH))
    xp2 = _out_proj(xp1, attn_p, conv_p, w_out_b, g_mix_post, tm=tp["tm"])

    qs, ks, vs = _qkv_proj(hs, w_in_b, tm=ts["tm"], with_block_means=False)
    state_t = jnp.transpose(state_conv[0], (1, 0, 2))
    conv_s, nstate_t = _conv_sample(hs, w_in_b, wc, state_t, tc=512)
    qs3, ks3, vs3 = (a.reshape(n_seq, *heads) for a in (qs, ks, vs))
    attn_s = _attn_sample(qs3, ks3, vs3, cache_k, cache_v, page_table)
    attn_s = attn_s.reshape(n_seq, ATTN_WIDTH).astype(BF16)
    xs2 = _out_proj(xs1, attn_s, conv_s, w_out_b, g_mix_post, tm=ts["tm"])

    xp3, hnp, xs3, hns = _ffn_ring(xp2, xs2, g_ffn2_pre, g_ffn2_post, g_ple, w2g, w2u, w2d,
                                   tm=tp["tm"], tf=tp["tf"])
    y_prompt = _ple(xp3, hnp, p_prompt[0, 0], w_pg, w_pp, tm=tp["tm"], tn=tp["tp"])
    y_sample = _ple(xs3, hns, p_sample[0, :, 0, :], w_pg, w_pp, tm=ts["tm"], tn=ts["tp"])

    return (y_prompt[None],
            y_sample[:, None, :],
            kp.reshape(1, 1, seq, *heads),
            vp.reshape(1, 1, seq, *heads),
            tail_p[V7X_SUBLANES - (CONV_K - 1):][None, None],
            ks.reshape(1, n_seq, 1, *heads),
            vs.reshape(1, n_seq, 1, *heads),
            jnp.transpose(nstate_t, (1, 0, 2))[None])
```

```python
import functools
import math

import jax
import jax.numpy as jnp
from jax import lax
from jax.experimental import pallas as pl
from jax.experimental.pallas import tpu as pltpu

F32 = jnp.float32
BF16 = jnp.bfloat16

D_MODEL = 2048
HEAD_DIM = 128
ATTN_HEADS = 8
ATTN_WIDTH = ATTN_HEADS * HEAD_DIM
CONV_WIDTH = D_MODEL - ATTN_WIDTH
CONV_K = 3
MOBA_BLOCK = 256
MOBA_TOPK = 3
RMS_EPS = 1e-6

V7X_SUBLANES = 8
V7X_LANES = 128
V7X_VMEM_LIMIT_BYTES = 56 * 1024 * 1024

FFN_EDGE_CHUNK = 256
ATTN_GROUP_LOG2 = 2
ATTN_GROUP = 1 << ATTN_GROUP_LOG2
ATTN_HEADS_PER_STEP = 2
SAMPLE_CHUNK = 8

NEG_BIG = -1e30
EXP2_SCALE = (HEAD_DIM ** -0.5) * math.log2(math.e)


def _params(*semantics):
    return pltpu.CompilerParams(dimension_semantics=semantics,
                                vmem_limit_bytes=V7X_VMEM_LIMIT_BYTES)


def _rms(x, g):
    var = jnp.mean(x * x, axis=-1, keepdims=True)
    return x * lax.rsqrt(var + RMS_EPS) * g


def _dot(a, b):
    return jnp.dot(a, b, preferred_element_type=F32)


def _dot_nt(a, b):
    return lax.dot_general(a, b, (((1,), (1,)), ((), ())), preferred_element_type=F32)


def _split3(x):
    hi = x.astype(BF16)
    r1 = x - hi.astype(F32)
    mid = r1.astype(BF16)
    lo = (r1 - mid.astype(F32)).astype(BF16)
    return hi, mid, lo


def _ffn_kernel(n_cast, x_ref, xs_ref, gpre_ref, gpost_ref, gnext_ref, wg_ref, wu_ref, wd_ref,
                *refs):
    n = n_cast
    w32_refs, (y_ref, hn_ref, ys_ref, hns_ref) = refs[:n], refs[n:n + 4]
    w16_refs = refs[n + 4:2 * n + 4]
    xn_ref, acc_ref, xns_ref, accs_ref = refs[2 * n + 4:]
    f = pl.program_id(1)

    def rows(x_ref, y_ref, hn_ref, xn_ref, acc_ref, side_casts=False):
        last = pl.num_programs(1) - 1
        n_rows = x_ref.shape[0]
        chunk = min(n_rows, FFN_EDGE_CHUNK)
        chunks = [slice(r, r + chunk) for r in range(0, n_rows, chunk)]

        def casts():
            if side_casts:
                for w32_ref, w16_ref in zip(w32_refs, w16_refs):
                    w16_ref[...] = w32_ref[...].astype(BF16)

        def mlp(xn):
            g = _dot(xn, wg_ref[...])
            u = _dot(xn, wu_ref[...])
            a = (g * (1.0 / (1.0 + jnp.exp(-g))) * u).astype(BF16)
            return _dot(a, wd_ref[...])

        @pl.when(f == 0)
        def _():
            casts()
            for c in chunks:
                xn = _rms(x_ref[c, :], gpre_ref[...]).astype(BF16)
                xn_ref[c, :] = xn
                acc_ref[c, :] = mlp(xn)

        @pl.when(jnp.logical_and(f > 0, f < last))
        def _():
            casts()
            acc_ref[...] += mlp(xn_ref[...])

        @pl.when(f == last)
        def _():
            casts()
            for c in chunks:
                acc = acc_ref[c, :] + mlp(xn_ref[c, :])
                y = x_ref[c, :] + 0.5 * _rms(acc, gpost_ref[...])
                y_ref[c, :] = y
                hn_ref[c, :] = _rms(y, gnext_ref[...]).astype(BF16)

    rows(x_ref, y_ref, hn_ref, xn_ref, acc_ref, side_casts=True)

    @pl.when(pl.program_id(0) == pl.num_programs(0) - 1)
    def _():
        rows(xs_ref, ys_ref, hns_ref, xns_ref, accs_ref)


def _cast_tiling(w, n_row_blocks, n_steps):
    rows, cols = w.shape
    steps = max(s for s in range(1, n_steps + 1)
                if cols % s == 0 and (cols // s) % V7X_LANES == 0)
    return pl.BlockSpec((rows // n_row_blocks, cols // steps),
                        lambda i, f: (i, jnp.minimum(f, steps - 1)))


def _ffn(x, xs, g_pre, g_post, g_next, wg, wu, wd, *, tm, tf, cast=()):
    m, d = x.shape
    ms = xs.shape[0]
    d_ff = wg.shape[1]
    grid = (m // tm, d_ff // tf)
    row = pl.BlockSpec((tm, d), lambda i, f: (i, 0))
    small = pl.BlockSpec((ms, d), lambda i, f: (0, 0))
    vec = pl.BlockSpec((1, d), lambda i, f: (0, 0))
    cast_specs = [_cast_tiling(w, *grid) for w in cast]
    return pl.pallas_call(
        functools.partial(_ffn_kernel, len(cast)),
        grid=grid,
        in_specs=[row, small, vec, vec, vec,
                  pl.BlockSpec((d, tf), lambda i, f: (0, f)),
                  pl.BlockSpec((d, tf), lambda i, f: (0, f)),
                  pl.BlockSpec((tf, d), lambda i, f: (f, 0)), *cast_specs],
        out_specs=[row, row, small, small, *cast_specs],
        out_shape=[jax.ShapeDtypeStruct((m, d), F32), jax.ShapeDtypeStruct((m, d), BF16),
                   jax.ShapeDtypeStruct((ms, d), F32), jax.ShapeDtypeStruct((ms, d), BF16),
                   *[jax.ShapeDtypeStruct(w.shape, BF16) for w in cast]],
        scratch_shapes=[pltpu.VMEM((tm, d), BF16), pltpu.VMEM((tm, d), F32),
                        pltpu.VMEM((ms, d), BF16), pltpu.VMEM((ms, d), F32)],
        compiler_params=_params("arbitrary", "arbitrary"),
        name="ffn",
    )(x, xs, g_pre, g_post, g_next, wg, wu, wd, *cast)


def _qkv_kernel(h_ref, wq_ref, wk_ref, wv_ref, q_ref, k_ref, v_ref, *extra):
    h = h_ref[...]
    q_ref[...] = _dot(h, wq_ref[...])
    k = _dot(h, wk_ref[...])
    k_ref[...] = k
    v_ref[...] = _dot(h, wv_ref[...])
    if extra:
        kb_ref, km_ref = extra
        kb_ref[...] = k.astype(BF16)
        blocks, width = km_ref.shape
        km_ref[...] = jnp.sum(k.reshape(blocks, MOBA_BLOCK, width), axis=1) * (1.0 / MOBA_BLOCK)


def _qkv_proj(hb, w_in, *, tm, with_block_means):
    m, d = hb.shape
    tn = ATTN_WIDTH
    wspec = lambda part: pl.BlockSpec((d, tn), lambda i: (0, part))
    ospec = pl.BlockSpec((tm, tn), lambda i: (i, 0))
    f32o = jax.ShapeDtypeStruct((m, tn), F32)
    out_specs, out_shape = [ospec] * 3, [f32o] * 3
    if with_block_means:
        bpt = tm // MOBA_BLOCK
        out_specs = out_specs + [ospec, pl.BlockSpec((None, bpt, tn), lambda i: (i, 0, 0))]
        out_shape = out_shape + [jax.ShapeDtypeStruct((m, tn), BF16),
                                 jax.ShapeDtypeStruct((m // tm, bpt, tn), F32)]
    return pl.pallas_call(
        _qkv_kernel,
        grid=(m // tm,),
        in_specs=[pl.BlockSpec((tm, d), lambda i: (i, 0)), wspec(0), wspec(1), wspec(2)],
        out_specs=out_specs,
        out_shape=out_shape,
        compiler_params=_params("arbitrary"),
        name="qkv_proj",
    )(hb, w_in, w_in, w_in)


def _conv_prompt_kernel(h_ref, wb_ref, wc_ref, wu_ref, wconv_ref, conv_ref, tail_ref, zbuf_ref):
    i = pl.program_id(1)
    tm = h_ref.shape[0]
    halo = V7X_SUBLANES

    @pl.when(i == 0)
    def _():
        zbuf_ref[0:halo, :] = jnp.zeros((halo, zbuf_ref.shape[1]), F32)

    h = h_ref[...]
    z = _dot(h, wc_ref[...]) * _dot(h, wu_ref[...])
    zbuf_ref[halo:halo + tm, :] = z
    w = wconv_ref[...]
    y = zbuf_ref[halo:halo + tm, :] * w[CONV_K - 1:CONV_K, :]
    for j in range(CONV_K - 1):
        off = halo - (CONV_K - 1) + j
        y = y + zbuf_ref[off:off + tm, :] * w[j:j + 1, :]
    conv_ref[...] = (_dot(h, wb_ref[...]) * y).astype(BF16)
    last = zbuf_ref[tm:tm + halo, :]
    zbuf_ref[0:halo, :] = last
    tail_ref[...] = last


def _conv_prompt(hb, w_in, w_conv, *, tm, tc):
    m, d = hb.shape
    nj = CONV_WIDTH // tc
    base = 3 * ATTN_WIDTH // tc
    wspec = lambda part: pl.BlockSpec((d, tc), lambda j, i: (0, base + part * nj + j))
    return pl.pallas_call(
        _conv_prompt_kernel,
        grid=(nj, m // tm),
        in_specs=[pl.BlockSpec((tm, d), lambda j, i: (i, 0)), wspec(0), wspec(1), wspec(2),
                  pl.BlockSpec((CONV_K, tc), lambda j, i: (0, j))],
        out_specs=[pl.BlockSpec((tm, tc), lambda j, i: (i, j)),
                   pl.BlockSpec((V7X_SUBLANES, tc), lambda j, i: (0, j))],
        out_shape=[jax.ShapeDtypeStruct((m, CONV_WIDTH), BF16),
                   jax.ShapeDtypeStruct((V7X_SUBLANES, CONV_WIDTH), F32)],
        scratch_shapes=[pltpu.VMEM((tm + V7X_SUBLANES, tc), F32)],
        compiler_params=_params("arbitrary", "arbitrary"),
        name="conv_prompt",
    )(hb, w_in, w_in, w_in, w_conv)


def _conv_sample_kernel(h_ref, wb_ref, wc_ref, wu_ref, wconv_ref, st_ref, conv_ref, nst_ref):
    h = h_ref[...]
    z = _dot(h, wc_ref[...]) * _dot(h, wu_ref[...])
    w = wconv_ref[...]
    y = z * w[CONV_K - 1:CONV_K, :]
    for j in range(CONV_K - 1):
        y = y + st_ref[j] * w[j:j + 1, :]
    conv_ref[...] = (_dot(h, wb_ref[...]) * y).astype(BF16)
    for j in range(CONV_K - 2):
        nst_ref[j] = st_ref[j + 1]
    nst_ref[CONV_K - 2] = z


def _conv_sample(hb, w_in, w_conv, state_t, *, tc):
    m, d = hb.shape
    nj = CONV_WIDTH // tc
    base = 3 * ATTN_WIDTH // tc
    wspec = lambda part: pl.BlockSpec((d, tc), lambda j: (0, base + part * nj + j))
    sspec = pl.BlockSpec((CONV_K - 1, m, tc), lambda j: (0, 0, j))
    return pl.pallas_call(
        _conv_sample_kernel,
        grid=(nj,),
        in_specs=[pl.BlockSpec((m, d), lambda j: (0, 0)), wspec(0), wspec(1), wspec(2),
                  pl.BlockSpec((CONV_K, tc), lambda j: (0, j)), sspec],
        out_specs=[pl.BlockSpec((m, tc), lambda j: (0, j)), sspec],
        out_shape=[jax.ShapeDtypeStruct((m, CONV_WIDTH), BF16),
                   jax.ShapeDtypeStruct((CONV_K - 1, m, CONV_WIDTH), F32)],
        compiler_params=_params("arbitrary"),
        name="conv_sample",
    )(hb, w_in, w_in, w_in, w_conv, state_t)


def _top_blocks(gate, n_past, axis_len):
    blk = lax.broadcasted_iota(jnp.int32, gate.shape, 0)
    past = blk < n_past
    g = jnp.where(past, gate, -jnp.inf)
    keep = jnp.zeros(gate.shape, F32)
    chosen = []
    for _ in range(MOBA_TOPK):
        mx = jnp.max(g, axis=0, keepdims=True)
        idx = jnp.min(jnp.where(g == mx, blk, axis_len), axis=0, keepdims=True)
        pick = blk == idx
        keep = jnp.where(pick, jnp.where(past, 1.0, keep), keep)
        g = jnp.where(pick, -jnp.inf, g)
        chosen.append(idx)
    return jnp.where(keep > 0.0, 0.0, NEG_BIG), chosen


def _top_blocks_bias(gate, n_past, axis_len):
    return _top_blocks(gate, n_past, axis_len)[0]


def _attn_prompt_kernel(q_ref, kb_ref, v_ref, km_ref, o_ref, kx_ref, vt_ref, vtb_ref, s_ref, sd_ref):
    i = pl.program_id(1)
    n_heads, n_grp = vt_ref.shape[0], vt_ref.shape[1]
    blk = MOBA_BLOCK
    grp = ATTN_GROUP
    gk = grp * blk
    nb = n_grp * grp
    dh = HEAD_DIM

    @pl.when(i == 0)
    def _():
        col = lax.broadcasted_iota(jnp.int32, (blk, dh), 1)
        for h in range(n_heads):
            for c in range(nb):
                rows = slice(c * blk, (c + 1) * blk)
                lane0 = (c % grp) * blk
                vt = v_ref[rows, h * dh:(h + 1) * dh].T.astype(BF16)
                vt_ref[h, c // grp, :, lane0:lane0 + blk] = vt
                vtb_ref[h, c] = vt
                kx_ref[h, rows, 0:dh] = kb_ref[rows, h * dh:(h + 1) * dh]
                kx_ref[h, rows, dh:2 * dh] = jnp.where(col == c, 1.0, 0.0).astype(BF16)

    def scores(h, g, q_ext):
        kg = kx_ref[h, pl.ds(pl.multiple_of(g * gk, gk), gk), :]
        return _dot_nt(kg, q_ext)

    def update(h, g, slot, state):
        m, l, acc = state
        s = s_ref[h, slot]
        m_new = jnp.maximum(m, jnp.max(s, axis=0, keepdims=True))
        p = jnp.exp2(s - m_new)
        alpha = jnp.exp2(m - m_new)
        l = alpha * l + jnp.sum(p, axis=0, keepdims=True)
        acc = acc * alpha + _dot(vt_ref[h, g], p.astype(BF16))
        return m_new, l, acc

    qbs, gates, s0s, q_exts, state = [], [], [], [], []
    for h in range(n_heads):
        q = q_ref[:, h * dh:(h + 1) * dh]
        qbs.append((q * EXP2_SCALE).astype(BF16))
        q_hi, q_mid, _ = _split3(q)
        k_hi, k_mid, _ = _split3(km_ref[:, h * dh:(h + 1) * dh])
        gates.append(_dot_nt(k_hi, q_hi) + (_dot_nt(k_hi, q_mid) + _dot_nt(k_mid, q_hi)))

    for h in range(n_heads):
        s0s.append(_dot_nt(kb_ref[0:gk, h * dh:(h + 1) * dh], qbs[h]))
        k_own = kb_ref[pl.ds(pl.multiple_of(i * blk, blk), blk), h * dh:(h + 1) * dh]
        s = _dot_nt(k_own, qbs[h])
        key_pos = lax.broadcasted_iota(jnp.int32, s.shape, 0)
        qry_pos = lax.broadcasted_iota(jnp.int32, s.shape, 1)
        sd_ref[h] = jnp.where(key_pos <= qry_pos, s, NEG_BIG)

    for h in range(n_heads):
        bias = _top_blocks_bias(gates[h], i, nb)
        s_ref[h, 0] = (s0s[h].reshape(grp, blk, blk) + bias[0:grp][:, None, :]).reshape(gk, blk)
        bias = jnp.concatenate([bias, jnp.zeros((dh - nb, blk), F32)], axis=0)
        q_exts.append(jnp.concatenate([qbs[h], bias.T.astype(BF16)], axis=1))
        state.append((jnp.full((1, blk), NEG_BIG, F32), jnp.zeros((1, blk), F32),
                      jnp.zeros((dh, blk), F32)))

    n_groups = lax.shift_right_logical(i + (grp - 1), ATTN_GROUP_LOG2)
    n_pairs = lax.shift_right_logical(n_groups, 1)

    def body(u, carry):
        g0 = 2 * u
        for h in range(n_heads):
            s_ref[h, 1] = scores(h, g0 + 1, q_exts[h])
        carry = tuple(update(h, g0, 0, carry[h]) for h in range(n_heads))
        for h in range(n_heads):
            s_ref[h, 0] = scores(h, jnp.minimum(g0 + 2, n_grp - 1), q_exts[h])
        return tuple(update(h, g0 + 1, 1, carry[h]) for h in range(n_heads))

    state = lax.fori_loop(0, n_pairs, body, tuple(state))

    def finish(with_group):
        for h in range(n_heads):
            m, l, acc = state[h]
            sd = sd_ref[h]
            m_new = jnp.maximum(m, jnp.max(sd, axis=0, keepdims=True))
            if with_group:
                s = s_ref[h, 0]
                m_new = jnp.maximum(m_new, jnp.max(s, axis=0, keepdims=True))
            pd = jnp.exp2(sd - m_new)
            alpha = jnp.exp2(m - m_new)
            l = alpha * l + jnp.sum(pd, axis=0, keepdims=True)
            acc = acc * alpha + _dot(vtb_ref[h, i], pd.astype(BF16))
            if with_group:
                p = jnp.exp2(s - m_new)
                l = l + jnp.sum(p, axis=0, keepdims=True)
                acc = acc + _dot(vt_ref[h, 2 * n_pairs], p.astype(BF16))
            o_ref[:, h * dh:(h + 1) * dh] = (acc / l).T.astype(BF16)

    odd = 2 * n_pairs < n_groups
    pl.when(odd)(lambda: finish(True))
    pl.when(jnp.logical_not(odd))(lambda: finish(False))


def _attn_prompt(q, kb, v, km):
    t = q.shape[0]
    nb = t // MOBA_BLOCK
    hp = ATTN_HEADS_PER_STEP
    assert nb % ATTN_GROUP == 0 and ATTN_HEADS % hp == 0
    w = hp * HEAD_DIM
    return pl.pallas_call(
        _attn_prompt_kernel,
        grid=(ATTN_HEADS // hp, nb),
        in_specs=[pl.BlockSpec((MOBA_BLOCK, w), lambda h, i: (i, h)),
                  pl.BlockSpec((t, w), lambda h, i: (0, h)),
                  pl.BlockSpec((t, w), lambda h, i: (0, h)),
                  pl.BlockSpec((nb, w), lambda h, i: (0, h))],
        out_specs=pl.BlockSpec((MOBA_BLOCK, w), lambda h, i: (i, h)),
        out_shape=jax.ShapeDtypeStruct((t, ATTN_WIDTH), BF16),
        scratch_shapes=[pltpu.VMEM((hp, t, 2 * HEAD_DIM), BF16),
                        pltpu.VMEM((hp, nb // ATTN_GROUP, HEAD_DIM, ATTN_GROUP * MOBA_BLOCK), BF16),
                        pltpu.VMEM((hp, nb, HEAD_DIM, MOBA_BLOCK), BF16),
                        pltpu.VMEM((hp, 2, ATTN_GROUP * MOBA_BLOCK, MOBA_BLOCK), F32),
                        pltpu.VMEM((hp, MOBA_BLOCK, MOBA_BLOCK), F32)],
        compiler_params=_params("arbitrary", "arbitrary"),
        name="attn_prompt",
    )(q, kb, v, km)


def _attn_sample_kernel(n_pages, pt_ref, q_ref, kn_ref, vn_ref, *refs):
    k_refs = refs[:n_pages]
    cv_ref, o_ref, s_ref, vbuf_ref, stat_ref, pick_ref, sem = refs[n_pages:]
    t = pl.program_id(0)
    n_seq = pl.num_programs(0) - 1
    page = k_refs[0].shape[0]
    pages_per_blk = MOBA_BLOCK // page
    n_blk = n_pages // pages_per_blk
    n_heads, sub = ATTN_HEADS, V7X_SUBLANES
    head_picks = [(h, r) for h in range(n_heads) for r in range(MOBA_TOPK)]

    ch = SAMPLE_CHUNK
    chunks = [(pg, c * ch) for pg in range(n_pages) for c in range(page // ch)]

    def slab_copy(slot, seq, h, r, blk_idx, part):
        page_id = pt_ref[seq, blk_idx * pages_per_blk + part]
        return pltpu.make_async_copy(cv_ref.at[0, page_id, :, h, :],
                                     vbuf_ref.at[slot, h, r, pl.ds(part * page, page), :],
                                     sem.at[slot])

    @pl.when(t < n_seq)
    def _():
        slot = jnp.bitwise_and(t, 1)
        q = q_ref[...]
        qe = q * EXP2_SCALE

        def score(k):
            return jnp.broadcast_to(jnp.sum(k * qe, axis=-1, keepdims=True), k.shape)

        ksum = [None] * n_blk
        smax = [None] * n_blk
        for pg, r0 in chunks:
            b = pg // pages_per_blk
            k = k_refs[pg][r0:r0 + ch]
            s = score(k)
            row0 = (pg * page + r0) * sub
            s_ref[slot, row0:row0 + ch * sub, :] = s.reshape(ch * sub, HEAD_DIM)
            ks, cm = jnp.sum(k, axis=0), jnp.max(s, axis=0)
            ksum[b] = ks if ksum[b] is None else ksum[b] + ks
            smax[b] = cm if smax[b] is None else jnp.maximum(smax[b], cm)

        kmean = jnp.concatenate([ks[None] for ks in ksum], axis=0) * (1.0 / MOBA_BLOCK)
        gate = jnp.sum(kmean * q[None], axis=-1, keepdims=True)
        bias, chosen = _top_blocks(gate, n_blk, n_blk)
        for n, (h, r) in enumerate(head_picks):
            blk_idx = jnp.minimum(chosen[r][0, h, 0], n_blk - 1)
            pick_ref[slot, n] = blk_idx
            for part in range(pages_per_blk):
                slab_copy(slot, t, h, r, blk_idx, part).start(priority=(n + part) % 2)

        s_new = score(kn_ref[...])
        m = s_new
        for b in range(n_blk):
            m = jnp.maximum(m, smax[b] + bias[b])
        stat_ref[slot, 0] = m
        stat_ref[slot, 1] = s_new

    @pl.when(t > 0)
    def _():
        slot = jnp.bitwise_and(t + 1, 1)
        picks = [(h, r, pick_ref[slot, n]) for n, (h, r) in enumerate(head_picks)]
        for h, r, blk_idx in picks:
            for part in range(pages_per_blk):
                slab_copy(slot, t - 1, h, r, blk_idx, part).wait()

        m, s_new = stat_ref[slot, 0], stat_ref[slot, 1]
        fold = lambda x: jnp.sum(x.reshape(MOBA_BLOCK // sub, sub, HEAD_DIM), axis=0)
        for h in range(n_heads):
            m_h = m[h:h + 1, :]
            p_new = jnp.exp2(s_new[h:h + 1, :] - m_h)
            l_h = jnp.zeros((sub, HEAD_DIM), F32)
            acc_h = jnp.zeros((sub, HEAD_DIM), F32)
            for hh, r, blk_idx in picks:
                if hh != h:
                    continue
                rows = pl.ds(blk_idx * (MOBA_BLOCK * sub) + h, MOBA_BLOCK, stride=sub)
                p = jnp.exp2(s_ref[slot, rows, :] - m_h)
                l_h = l_h + fold(p)
                acc_h = acc_h + fold(p * vbuf_ref[slot, h, r])
            l_row = jnp.sum(l_h, axis=0, keepdims=True) + p_new
            acc_row = jnp.sum(acc_h, axis=0, keepdims=True) + p_new * vn_ref[h:h + 1, :]
            o_ref[h:h + 1, :] = acc_row / l_row


def _attn_sample(q, k_new, v_new, cache_k, cache_v, page_table):
    n_seq, n_pages = page_table.shape
    page = cache_k.shape[2]
    assert MOBA_BLOCK % page == 0 and (n_pages * page) % MOBA_BLOCK == 0
    assert n_pages * page // MOBA_BLOCK >= MOBA_TOPK and ATTN_HEADS == V7X_SUBLANES
    lanes = (ATTN_HEADS, HEAD_DIM)
    last = n_seq - 1
    key_row = pl.BlockSpec((None,) + lanes, lambda t, pt: (jnp.minimum(t, last), 0, 0))
    val_row = pl.BlockSpec((None,) + lanes, lambda t, pt: (jnp.maximum(t - 1, 0), 0, 0))
    page_spec = lambda pg: pl.BlockSpec((None, None, page) + lanes,
                                        lambda t, pt: (0, pt[jnp.minimum(t, last), pg], 0, 0, 0))
    return pl.pallas_call(
        functools.partial(_attn_sample_kernel, n_pages),
        grid_spec=pltpu.PrefetchScalarGridSpec(
            num_scalar_prefetch=1,
            grid=(n_seq + 1,),
            in_specs=([key_row, key_row, val_row] + [page_spec(pg) for pg in range(n_pages)]
                      + [pl.BlockSpec(memory_space=pl.ANY)]),
            out_specs=val_row,
            scratch_shapes=[pltpu.VMEM((2, n_pages * page * ATTN_HEADS, HEAD_DIM), F32),
                            pltpu.VMEM((2, ATTN_HEADS, MOBA_TOPK, MOBA_BLOCK, HEAD_DIM), F32),
                            pltpu.VMEM((2, 2) + lanes, F32),
                            pltpu.SMEM((2, ATTN_HEADS * MOBA_TOPK), jnp.int32),
                            pltpu.SemaphoreType.DMA((2,))],
        ),
        out_shape=jax.ShapeDtypeStruct((n_seq,) + lanes, F32),
        compiler_params=_params("arbitrary"),
        name="attn_sample",
    )(page_table, q, k_new, v_new, *([cache_k] * n_pages), cache_v)


def _out_proj_kernel(x_ref, a_ref, c_ref, w_ref, g_ref, y_ref):
    mix = _dot(a_ref[...], w_ref[0:ATTN_WIDTH, :]) + _dot(c_ref[...], w_ref[ATTN_WIDTH:, :])
    y_ref[...] = x_ref[...] + _rms(mix, g_ref[...])


def _out_proj(x, attn, conv, w_out, g_post, *, tm):
    m, d = x.shape
    row = lambda width: pl.BlockSpec((tm, width), lambda i: (i, 0))
    return pl.pallas_call(
        _out_proj_kernel,
        grid=(m // tm,),
        in_specs=[row(d), row(ATTN_WIDTH), row(CONV_WIDTH),
                  pl.BlockSpec((d, d), lambda i: (0, 0)), pl.BlockSpec((1, d), lambda i: (0, 0))],
        out_specs=row(d),
        out_shape=jax.ShapeDtypeStruct((m, d), F32),
        compiler_params=_params("arbitrary"),
        name="out_proj",
    )(x, attn, conv, w_out, g_post)


def _ple_kernel(x_ref, hn_ref, p_ref, wg_ref, wp_ref, y_ref):
    gate = _dot(hn_ref[...], wg_ref[...])
    emb = _dot(p_ref[...].astype(BF16), wp_ref[...])
    y_ref[...] = x_ref[...] + (1.0 / (1.0 + jnp.exp(-gate))) * emb


def _ple(x, hn, p, w_gate, w_proj, *, tm, tn):
    m, d = x.shape
    pd = p.shape[1]
    return pl.pallas_call(
        _ple_kernel,
        grid=(m // tm, d // tn),
        in_specs=[pl.BlockSpec((tm, tn), lambda i, j: (i, j)),
                  pl.BlockSpec((tm, d), lambda i, j: (i, 0)),
                  pl.BlockSpec((tm, pd), lambda i, j: (i, 0)),
                  pl.BlockSpec((d, tn), lambda i, j: (0, j)),
                  pl.BlockSpec((pd, tn), lambda i, j: (0, j))],
        out_specs=pl.BlockSpec((tm, tn), lambda i, j: (i, j)),
        out_shape=jax.ShapeDtypeStruct((m, d), F32),
        compiler_params=_params("arbitrary", "arbitrary"),
        name="ple",
    )(x, hn, p, w_gate, w_proj)


def _tiles(m):
    tm = min(m, 512)
    return dict(tm=tm, tf=512, tn=ATTN_WIDTH, tp=D_MODEL)


def kernel(x_prompt, x_sample, cache_k, cache_v, state_conv, page_table, p_prompt, p_sample, w_in, w_out, w_conv, w_ffn1_gate, w_ffn1_up, w_ffn1_down, w_ffn2_gate, w_ffn2_up, w_ffn2_down, w_ple_gate, w_ple_proj, g_ffn1_pre, g_ffn1_post, g_mix_pre, g_mix_post, g_ffn2_pre, g_ffn2_post, g_ple):
    depth = w_in.shape[0]
    assert depth == 1 and x_prompt.shape[0] == 1 and x_sample.shape[1] == 1
    n_seq = x_sample.shape[0]
    seq = x_prompt.shape[1]
    n_phys, page = cache_k.shape[1], cache_k.shape[2]

    bf = lambda w: w[0].astype(BF16)
    w1g, w1u, w1d = bf(w_ffn1_gate), bf(w_ffn1_up), bf(w_ffn1_down)
    wc = w_conv[0]

    tp, ts = _tiles(seq), _tiles(n_seq)
    heads = (ATTN_HEADS, HEAD_DIM)

    later = (w_ffn2_gate[0], w_ffn2_up[0], w_ffn2_down[0], w_in[0], w_out[0],
             w_ple_gate[0], w_ple_proj[0])
    xp1, hp, xs1, hs, w2g, w2u, w2d, w_in_b, w_out_b, w_pg, w_pp = _ffn(
        x_prompt[0], x_sample[:, 0, :], g_ffn1_pre, g_ffn1_post, g_mix_pre,
        w1g, w1u, w1d, tm=tp["tm"], tf=tp["tf"], cast=later)

    qp, kp, vp, kbp, kmp = _qkv_proj(hp, w_in_b, tm=tp["tm"], with_block_means=True)
    conv_p, tail_p = _conv_prompt(hp, w_in_b, wc, tm=512, tc=512)
    attn_p = _attn_prompt(qp, kbp, vp, kmp.reshape(seq // MOBA_BLOCK, ATTN_WIDTH))
    xp2 = _out_proj(xp1, attn_p, conv_p, w_out_b, g_mix_post, tm=tp["tm"])

    qs, ks, vs = _qkv_proj(hs, w_in_b, tm=ts["tm"], with_block_means=False)
    state_t = jnp.transpose(state_conv[0], (1, 0, 2))
    conv_s, nstate_t = _conv_sample(hs, w_in_b, wc, state_t, tc=512)
    qs3, ks3, vs3 = (a.reshape(n_seq, *heads) for a in (qs, ks, vs))
    attn_s = _attn_sample(qs3, ks3, vs3, cache_k, cache_v, page_table)
    attn_s = attn_s.reshape(n_seq, ATTN_WIDTH).astype(BF16)
    xs2 = _out_proj(xs1, attn_s, conv_s, w_out_b, g_mix_post, tm=ts["tm"])

    xp3, hnp, xs3, hns = _ffn(xp2, xs2, g_ffn2_pre, g_ffn2_post, g_ple, w2g, w2u, w2d,
                              tm=tp["tm"], tf=tp["tf"])
    y_prompt = _ple(xp3, hnp, p_prompt[0, 0], w_pg, w_pp, tm=tp["tm"], tn=tp["tp"])
    y_sample = _ple(xs3, hns, p_sample[0, :, 0, :], w_pg, w_pp, tm=ts["tm"], tn=ts["tp"])

    return (y_prompt[None],
            y_sample[:, None, :],
            kp.reshape(1, 1, seq, *heads),
            vp.reshape(1, 1, seq, *heads),
            tail_p[V7X_SUBLANES - (CONV_K - 1):][None, None],
            ks.reshape(1, n_seq, 1, *heads),
            vs.reshape(1, n_seq, 1, *heads),
            jnp.transpose(nstate_t, (1, 0, 2))[None])
```
